```python
import jax, jax.numpy as jnp
from jax import lax
import numpy as np

D_MODEL = 2048
BATCH = 16
SEQ = 2048
DEPTH = 4

HEAD_DIM = 128
N_HEADS = D_MODEL // HEAD_DIM
POOL_WINDOWS = (2, 4, 8, 16)
POOL_GROUPS = len(POOL_WINDOWS)
CONV_HEADS = (N_HEADS - POOL_GROUPS) // 2
GATE_HEADS = N_HEADS - POOL_GROUPS - CONV_HEADS
CONV_CH = CONV_HEADS * HEAD_DIM
POOL_CH = POOL_GROUPS * HEAD_DIM
GATE_CH = GATE_HEADS * HEAD_DIM
MIX_WIDTH = CONV_CH + POOL_CH + GATE_CH
IN_COLS = 2 * CONV_CH + POOL_CH + 2 * GATE_CH
CONV_WIDTH = 31
CHUNK = 128
N_EXPERTS = 32
TOP_K = 4
D_FF_EXPERT = 768
SWIGLU_ALPHA = 1.702
SWIGLU_LIMIT = 7.0
EXPERT_BLOCK = 256
EPS = 1e-5

kernel_name = 'hybrid_conv_pool_gate_moe_block'


def _rms(x):
    xf = x.astype(jnp.float32)
    return (xf * lax.rsqrt(jnp.mean(xf * xf, axis=-1, keepdims=True) + EPS)).astype(x.dtype)


def _layernorm(x, g, b):
    xf = x.astype(jnp.float32)
    mu = jnp.mean(xf, axis=-1, keepdims=True)
    var = jnp.mean(jnp.square(xf - mu), axis=-1, keepdims=True)
    return ((xf - mu) * lax.rsqrt(var + EPS)).astype(x.dtype) * g + b


def _conv_module(za, dw_w, dw_b, ln_g, ln_b, pw_w):
    a = za[..., :CONV_CH] * jax.nn.sigmoid(za[..., CONV_CH:])
    a = lax.conv_general_dilated(
        a, dw_w[:, None, :], window_strides=(1,), padding=((CONV_WIDTH - 1, 0),),
        dimension_numbers=('NWC', 'WIO', 'NWC'), feature_group_count=CONV_CH) + dw_b
    a = jax.nn.silu(_layernorm(a, ln_g, ln_b))
    return a @ pw_w


def _pool_mixer(p, pool_w, pool_scale):
    B, S, _ = p.shape
    pg = p.reshape(B, S, POOL_GROUPS, HEAD_DIM)
    pf = pg.astype(jnp.float32)
    cs = jnp.cumsum(pf, axis=1)
    pos = jnp.arange(1, S + 1, dtype=jnp.float32)
    means = []
    for g, w in enumerate(POOL_WINDOWS):
        cg = cs[:, :, g]
        lagged = jnp.pad(cg, ((0, 0), (w, 0), (0, 0)))[:, :S]
        cnt = jnp.minimum(pos, float(w))[None, :, None]
        means.append((cg - lagged) / cnt)
    diff = (jnp.stack(means, axis=2) - pf).astype(p.dtype)
    y = jnp.einsum('bsgc,gcd->bsgd', diff, pool_w)
    return y.reshape(B, S, POOL_CH) * pool_scale


def _gate_mixer(zc, ln_g, ln_b, sg_w, sg_b):
    B, S, _ = zc.shape
    zc = jax.nn.gelu(zc)
    u, v = zc[..., :GATE_CH], zc[..., GATE_CH:]
    v = _layernorm(v, ln_g, ln_b).reshape(B, S // CHUNK, CHUNK, GATE_HEADS, HEAD_DIM)
    w = sg_w * jnp.tril(jnp.ones((CHUNK, CHUNK), sg_w.dtype))
    s = jnp.einsum('htq,bnqhd->bnthd', w, v) + sg_b.T[None, None, :, :, None]
    return u * s.reshape(B, S, GATE_CH)


def _clamped_swiglu(z):
    glu = jnp.minimum(z[..., :D_FF_EXPERT], SWIGLU_LIMIT)
    lin = jnp.clip(z[..., D_FF_EXPERT:], -SWIGLU_LIMIT, SWIGLU_LIMIT)
    return glu * jax.nn.sigmoid(SWIGLU_ALPHA * glu) * (lin + 1.0)


def _moe(h, router_w, router_b, w1, b1, w2, b2):
    T, D = h.shape
    TK = T * TOP_K
    logits = h.astype(jnp.float32) @ router_w.astype(jnp.float32) + router_b.astype(jnp.float32)
    top_vals, top_idx = lax.top_k(logits, TOP_K)
    gates = jax.nn.softmax(top_vals, axis=-1)
    flat_e = top_idx.reshape(-1).astype(jnp.int32)
    flat_tok = jnp.repeat(jnp.arange(T, dtype=jnp.int32), TOP_K)
    flat_g = gates.reshape(-1)
    order = jnp.argsort(flat_e)
    e_sorted, tok_sorted, g_sorted = flat_e[order], flat_tok[order], flat_g[order]
    counts = jnp.bincount(flat_e, length=N_EXPERTS).astype(jnp.int32)
    starts = jnp.cumsum(counts) - counts
    padded = (counts + EXPERT_BLOCK - 1) // EXPERT_BLOCK * EXPERT_BLOCK
    pad_ends = jnp.cumsum(padded)
    pad_starts = pad_ends - padded
    dest = pad_starts[e_sorted] + (jnp.arange(TK, dtype=jnp.int32) - starts[e_sorted])
    n_blocks = -(-TK // EXPERT_BLOCK) + N_EXPERTS
    rows = n_blocks * EXPERT_BLOCK
    row_tok = jnp.zeros((rows,), jnp.int32).at[dest].set(tok_sorted)
    row_gate = jnp.zeros((rows,), jnp.float32).at[dest].set(g_sorted)
    block_e = jnp.minimum(
        jnp.searchsorted(pad_ends, jnp.arange(n_blocks, dtype=jnp.int32) * EXPERT_BLOCK, side='right'),
        N_EXPERTS - 1).astype(jnp.int32)

    def body(y, blk):
        tok, g, e = blk
        xb = h[tok]
        a = _clamped_swiglu(xb @ w1[e] + b1[e])
        o = a @ w2[e] + b2[e]
        return y.at[tok].add(o * g[:, None].astype(o.dtype)), None

    y0 = jnp.zeros((T, D), h.dtype)
    y, _ = lax.scan(body, y0, (row_tok.reshape(n_blocks, EXPERT_BLOCK),
                               row_gate.reshape(n_blocks, EXPERT_BLOCK), block_e))
    return y


def setup_inputs(seed: int = 0) -> dict:
    key = jax.random.key(seed)
    ks = jax.random.split(key, 25)
    L, D, F, E = DEPTH, D_MODEL, D_FF_EXPERT, N_EXPERTS

    def nrm(k, shape, scale):
        return jax.random.normal(k, shape, jnp.float32) * scale

    return {
        'x': nrm(ks[0], (BATCH, SEQ, D), 1.0),
        'c': nrm(ks[1], (BATCH, D), 1.0),
        'ada_w': nrm(ks[2], (L, D, 6 * D), 0.5 * D ** -0.5),
        'ada_b': nrm(ks[3], (L, 6 * D), 0.02),
        'w_in': nrm(ks[4], (L, D, IN_COLS), D ** -0.5),
        'conv_dw_w': nrm(ks[5], (L, CONV_WIDTH, CONV_CH), CONV_WIDTH ** -0.5),
        'conv_dw_b': nrm(ks[6], (L, CONV_CH), 0.02),
        'conv_ln_g': 1.0 + nrm(ks[7], (L, CONV_CH), 0.02),
        'conv_ln_b': nrm(ks[8], (L, CONV_CH), 0.02),
        'conv_pw_w': nrm(ks[9], (L, CONV_CH, CONV_CH), CONV_CH ** -0.5),
        'pool_w': nrm(ks[10], (L, POOL_GROUPS, HEAD_DIM, HEAD_DIM), HEAD_DIM ** -0.5),
        'pool_scale': 1.0 + nrm(ks[11], (L, POOL_CH), 0.1),
        'sg_ln_g': 1.0 + nrm(ks[12], (L, GATE_CH), 0.02),
        'sg_ln_b': nrm(ks[13], (L, GATE_CH), 0.02),
        'sg_w': nrm(ks[14], (L, GATE_HEADS, CHUNK, CHUNK), CHUNK ** -0.5),
        'sg_b': 1.0 + nrm(ks[15], (L, GATE_HEADS, CHUNK), 0.02),
        'mix_norm_g': 1.0 + nrm(ks[16], (L, MIX_WIDTH), 0.02),
        'w_out': nrm(ks[17], (L, MIX_WIDTH, D), MIX_WIDTH ** -0.5),
        'router_w': nrm(ks[18], (L, D, E), D ** -0.5),
        'router_b': nrm(ks[19], (L, E), 0.01),
        'moe_w1': nrm(ks[20], (L, E, D, 2 * F), D ** -0.5),
        'moe_b1': nrm(ks[21], (L, E, 2 * F), 0.02),
        'moe_w2': nrm(ks[22], (L, E, F, D), F ** -0.5),
        'moe_b2': nrm(ks[23], (L, E, D), 0.02),
        'final_g': 1.0 + nrm(ks[24], (D,), 0.02),
    }


def reference(x, c, ada_w, ada_b, w_in, conv_dw_w, conv_dw_b, conv_ln_g, conv_ln_b, conv_pw_w,
              pool_w, pool_scale, sg_ln_g, sg_ln_b, sg_w, sg_b, mix_norm_g, w_out,
              router_w, router_b, moe_w1, moe_b1, moe_w2, moe_b2, final_g):
    B, S, D = x.shape
    c_act = jax.nn.silu(c)
    for l in range(DEPTH):
        mod = c_act @ ada_w[l] + ada_b[l]
        sh1, sc1, g1, sh2, sc2, g2 = jnp.split(mod, 6, axis=-1)

        h = _rms(x) * (1.0 + sc1[:, None]) + sh1[:, None]
        z = h @ w_in[l]
        za = z[..., :2 * CONV_CH]
        zb = z[..., 2 * CONV_CH:2 * CONV_CH + POOL_CH]
        zc = z[..., 2 * CONV_CH + POOL_CH:]
        ya = _conv_module(za, conv_dw_w[l], conv_dw_b[l], conv_ln_g[l], conv_ln_b[l], conv_pw_w[l])
        yb = _pool_mixer(zb, pool_w[l], pool_scale[l])
        yc = _gate_mixer(zc, sg_ln_g[l], sg_ln_b[l], sg_w[l], sg_b[l])
        y = jnp.concatenate([_rms(ya), _rms(yb), _rms(yc)], axis=-1) * mix_norm_g[l]
        x = x + g1[:, None] * (y @ w_out[l])

        h = _rms(x) * (1.0 + sc2[:, None]) + sh2[:, None]
        f = _moe(h.reshape(B * S, D), router_w[l], router_b[l],
                 moe_w1[l], moe_b1[l], moe_w2[l], moe_b2[l]).reshape(B, S, D)
        x = x + g2[:, None] * f
    return _rms(x) * final_g
```

```python
import functools

import jax
import jax.numpy as jnp
from jax import lax
from jax.experimental import pallas as pl
from jax.experimental.pallas import tpu as pltpu

f32 = jnp.float32
bf16 = jnp.bfloat16
u32 = jnp.uint32
i32 = jnp.int32

D = 2048
HEAD = 128
CONV_CH = 768
POOL_CH = 512
GATE_CH = 768
IN_COLS = 2 * CONV_CH + POOL_CH + 2 * GATE_CH
POOL_OFF = 2 * CONV_CH
GATE_OFF = 2 * CONV_CH + POOL_CH
POOL_WINDOWS = (2, 4, 8, 16)
CONV_W = 31
CHUNK = 128
N_EXP = 32
TOP_K = 4
F_EXP = 768
SWIGLU_ALPHA = 1.702
SWIGLU_LIMIT = 7.0
EPS = 1e-5

HALO = 32
CONV_ROWS = 64
PACK_W = D // 2
PACK_S = PACK_W // 128
VMEM_LIMIT = 56 * 1024 * 1024


def _sigmoid(v):
    return 1.0 / (1.0 + jnp.exp(-v))


def _rms(v):
    return v * lax.rsqrt(jnp.mean(v * v, axis=-1, keepdims=True) + EPS)


def _layernorm(v, g, b):
    mu = jnp.mean(v, axis=-1, keepdims=True)
    d = v - mu
    var = jnp.mean(d * d, axis=-1, keepdims=True)
    return d * lax.rsqrt(var + EPS) * g + b


def _pack_pair(lo, hi):
    lo_bits = lax.bitcast_convert_type(lo.astype(bf16).astype(f32), u32)
    hi_bits = lax.bitcast_convert_type(hi.astype(bf16).astype(f32), u32)
    return (lo_bits >> 16) | (hi_bits & jnp.uint32(0xFFFF0000))


def _unpack_pair(w):
    lo = lax.bitcast_convert_type(w << 16, f32)
    hi = lax.bitcast_convert_type(w & jnp.uint32(0xFFFF0000), f32)
    return lo, hi


def _params(n_axes):
    return pltpu.CompilerParams(dimension_semantics=("arbitrary",) * n_axes,
                                vmem_limit_bytes=VMEM_LIMIT)


def _resident(shape):
    nd = len(shape)
    return pl.BlockSpec(shape, lambda *_: (0,) * nd, pipeline_mode=pl.Buffered(1))


def _ada_kernel(c_ref, w_ref, b_ref, o_ref):
    c = c_ref[...]
    ca = (c * _sigmoid(c)).astype(bf16)
    o_ref[0] = jnp.dot(ca, w_ref[0].astype(bf16), preferred_element_type=f32) + b_ref[0]


def _ada_call(c, ada_w, ada_b):
    L, _, n6 = ada_w.shape
    B = c.shape[0]
    tn = 1024
    return pl.pallas_call(
        _ada_kernel,
        grid=(L, n6 // tn),
        in_specs=[pl.BlockSpec((B, D), lambda l, n: (0, 0)),
                  pl.BlockSpec((1, D, tn), lambda l, n: (l, 0, n)),
                  pl.BlockSpec((1, 1, tn), lambda l, n: (l, 0, n))],
        out_specs=pl.BlockSpec((1, B, tn), lambda l, n: (l, 0, n)),
        out_shape=jax.ShapeDtypeStruct((L, B, n6), f32),
        compiler_params=_params(2),
        name="ada_mod",
    )(c, ada_w, ada_b.reshape(L, 1, n6))


def _inproj_kernel(x_ref, sc_ref, sh_ref, w_ref, z_ref):
    h = _rms(x_ref[0]) * (1.0 + sc_ref[0]) + sh_ref[0]
    z_ref[0] = jnp.dot(h.astype(bf16), w_ref[...], preferred_element_type=f32).astype(bf16)


def _inproj_call(x, sc, sh, w_in_b, tm):
    B, S, _ = x.shape
    return pl.pallas_call(
        _inproj_kernel,
        grid=(B, S // tm),
        in_specs=[pl.BlockSpec((1, tm, D), lambda b, j: (b, j, 0)),
                  pl.BlockSpec((1, 1, D), lambda b, j: (b, 0, 0)),
                  pl.BlockSpec((1, 1, D), lambda b, j: (b, 0, 0)),
                  _resident((D, IN_COLS))],
        out_specs=pl.BlockSpec((1, tm, IN_COLS), lambda b, j: (b, j, 0)),
        out_shape=jax.ShapeDtypeStruct((B, S, IN_COLS), bf16),
        compiler_params=_params(2),
        name="in_proj",
    )(x, sc, sh, w_in_b)


def _mixer_kernel(z_ref, dww_ref, dwb_ref, clg_ref, clb_ref, pw_ref, poolw_ref, pscale_ref,
                  slg_ref, slb_ref, sgw_ref, sgbt_ref, mixg_ref, y_ref, abuf, pbuf, cbuf):
    j = pl.program_id(1)
    tm = y_ref.shape[1]

    @pl.when(j == 0)
    def _():
        abuf[0:HALO, :] = jnp.zeros((HALO, CONV_CH), f32)
        pbuf[0:HALO, :] = jnp.zeros((HALO, POOL_CH), f32)

    za = z_ref[0, :, 0:CONV_CH].astype(f32)
    zg = z_ref[0, :, CONV_CH:2 * CONV_CH].astype(f32)
    abuf[HALO:HALO + tm, :] = za * _sigmoid(zg)
    for r0 in range(0, tm, CONV_ROWS):
        for c0 in range(0, CONV_CH, 128):
            acc = jnp.broadcast_to(dwb_ref[:, c0:c0 + 128], (CONV_ROWS, 128))
            for t in range(CONV_W):
                off = HALO - (CONV_W - 1) + t + r0
                acc = acc + dww_ref[t:t + 1, c0:c0 + 128] * abuf[off:off + CONV_ROWS, c0:c0 + 128]
            cbuf[r0:r0 + CONV_ROWS, c0:c0 + 128] = acc
    abuf[0:HALO, :] = abuf[tm:tm + HALO, :]
    an = _layernorm(cbuf[...], clg_ref[...], clb_ref[...])
    an = an * _sigmoid(an)
    ya = _rms(jnp.dot(an.astype(bf16), pw_ref[...], preferred_element_type=f32))
    y_ref[0, :, 0:CONV_CH] = (ya * mixg_ref[:, 0:CONV_CH]).astype(bf16)

    p = z_ref[0, :, POOL_OFF:POOL_OFF + POOL_CH].astype(f32)
    pbuf[HALO:HALO + tm, :] = p
    pos = (j * tm + 1 + lax.broadcasted_iota(i32, (tm, 1), 0)).astype(f32)
    ybs = []
    for g, w in enumerate(POOL_WINDOWS):
        c0 = g * HEAD
        win = pbuf[HALO:HALO + tm, c0:c0 + HEAD]
        for i in range(1, w):
            win = win + pbuf[HALO - i:HALO - i + tm, c0:c0 + HEAD]
        diff = win / jnp.minimum(pos, float(w)) - p[:, c0:c0 + HEAD]
        yg = jnp.dot(diff.astype(bf16), poolw_ref[g], preferred_element_type=f32)
        ybs.append(yg * pscale_ref[:, c0:c0 + HEAD])
    pbuf[0:HALO, :] = pbuf[tm:tm + HALO, :]
    yb = _rms(jnp.concatenate(ybs, axis=-1))
    y_ref[0, :, CONV_CH:CONV_CH + POOL_CH] = (yb * mixg_ref[:, CONV_CH:CONV_CH + POOL_CH]).astype(bf16)

    zc = z_ref[0, :, GATE_OFF:GATE_OFF + 2 * GATE_CH].astype(f32)
    gl = 0.5 * zc * (1.0 + jnp.tanh(0.7978845608028654 * (zc + 0.044715 * zc * zc * zc)))
    u = gl[:, 0:GATE_CH]
    vn = _layernorm(gl[:, GATE_CH:], slg_ref[...], slb_ref[...]).astype(bf16)
    causal = (lax.broadcasted_iota(i32, (CHUNK, CHUNK), 1)
              <= lax.broadcasted_iota(i32, (CHUNK, CHUNK), 0))
    for hh in range(GATE_CH // HEAD):
        c0 = hh * HEAD
        wm = jnp.where(causal, sgw_ref[hh], 0.0).astype(bf16)
        bias = sgbt_ref[:, hh:hh + 1]
        for n in range(tm // CHUNK):
            r0 = n * CHUNK
            s = jnp.dot(wm, vn[r0:r0 + CHUNK, c0:c0 + HEAD], preferred_element_type=f32) + bias
            cbuf[r0:r0 + CHUNK, c0:c0 + HEAD] = u[r0:r0 + CHUNK, c0:c0 + HEAD] * s
    yc = _rms(cbuf[...])
    y_ref[0, :, CONV_CH + POOL_CH:] = (yc * mixg_ref[:, CONV_CH + POOL_CH:]).astype(bf16)


def _mixer_call(z, dww, dwb, clg, clb, pw_b, poolw_b, pscale, slg, slb, sgw, sgbt, mixg, tm):
    B, S, _ = z.shape
    return pl.pallas_call(
        _mixer_kernel,
        grid=(B, S // tm),
        in_specs=[pl.BlockSpec((1, tm, IN_COLS), lambda b, j: (b, j, 0)),
                  _resident((CONV_W, CONV_CH)), _resident((1, CONV_CH)),
                  _resident((1, CONV_CH)), _resident((1, CONV_CH)),
                  _resident((CONV_CH, CONV_CH)),
                  _resident((len(POOL_WINDOWS), HEAD, HEAD)), _resident((1, POOL_CH)),
                  _resident((1, GATE_CH)), _resident((1, GATE_CH)),
                  _resident((GATE_CH // HEAD, CHUNK, CHUNK)), _resident((CHUNK, GATE_CH // HEAD)),
                  _resident((1, D))],
        out_specs=pl.BlockSpec((1, tm, D), lambda b, j: (b, j, 0)),
        out_shape=jax.ShapeDtypeStruct((B, S, D), bf16),
        scratch_shapes=[pltpu.VMEM((HALO + tm, CONV_CH), f32),
                        pltpu.VMEM((HALO + tm, POOL_CH), f32),
                        pltpu.VMEM((tm, CONV_CH), f32)],
        compiler_params=_params(2),
        name="mixers",
    )(z, dww, dwb, clg, clb, pw_b, poolw_b, pscale, slg, slb, sgw, sgbt, mixg)


def _outproj_router_kernel(x_ref, y_ref, wout_ref, g1_ref, sc_ref, sh_ref, wrt_ref, rb_ref,
                           xo_ref, hp_ref, idx_ref, gate_ref, rank_ref, cnt_ref, carry):
    tm = x_ref.shape[1]

    @pl.when((pl.program_id(0) == 0) & (pl.program_id(1) == 0))
    def _():
        carry[...] = jnp.zeros_like(carry)

    x = x_ref[0] + g1_ref[0] * jnp.dot(y_ref[0], wout_ref[...], preferred_element_type=f32)
    xo_ref[0] = x
    h = _rms(x) * (1.0 + sc_ref[0]) + sh_ref[0]
    h_hi = h.astype(bf16)
    h_hi32 = h_hi.astype(f32)
    h_lo = (h - h_hi32).astype(bf16)
    for s in range(PACK_S):
        hp_ref[:, s, :] = _pack_pair(h_hi32[:, 256 * s:256 * s + 128],
                                     h_hi32[:, 256 * s + 128:256 * s + 256])

    wr = wrt_ref[...]
    w_hi = wr.astype(bf16)
    w_lo = (wr - w_hi.astype(f32)).astype(bf16)
    nt = (((1,), (1,)), ((), ()))
    lg = (lax.dot_general(w_hi, h_hi, nt, preferred_element_type=f32)
          + lax.dot_general(w_hi, h_lo, nt, preferred_element_type=f32)
          + lax.dot_general(w_lo, h_hi, nt, preferred_element_type=f32)
          + rb_ref[...])

    e_iota = lax.broadcasted_iota(i32, (N_EXP, tm), 0)
    vals, idxs, hots = [], [], []
    for _ in range(TOP_K):
        m = jnp.max(lg, axis=0, keepdims=True)
        idx = jnp.min(jnp.where(lg == m, e_iota, N_EXP), axis=0, keepdims=True)
        hot = e_iota == idx
        lg = jnp.where(hot, -jnp.inf, lg)
        vals.append(m)
        idxs.append(idx)
        hots.append(hot)
    exps = [jnp.exp(v - vals[0]) for v in vals]
    denom = exps[0] + exps[1] + exps[2] + exps[3]
    idx_ref[...] = jnp.concatenate(idxs, axis=0)
    gate_ref[...] = jnp.concatenate([e / denom for e in exps], axis=0)

    sel = jnp.zeros((N_EXP, tm), f32)
    for hot in hots:
        sel = sel + jnp.where(hot, 1.0, 0.0)
    upper = jnp.where(lax.broadcasted_iota(i32, (tm, tm), 0) <= lax.broadcasted_iota(i32, (tm, tm), 1),
                      1.0, 0.0).astype(bf16)
    cum = jnp.dot(sel.astype(bf16), upper, preferred_element_type=f32)
    before = cum - sel + carry[:, 0:1]
    ranks = [jnp.sum(jnp.where(hot, before, 0.0), axis=0, keepdims=True) for hot in hots]
    rank_ref[...] = jnp.concatenate(ranks, axis=0).astype(i32)
    carry[...] = carry[...] + cum[:, tm - 1:tm]
    cnt_ref[...] = carry[...]


def _outproj_router_call(x, y, w_out_b, g1, sc2, sh2, wrt, rb, tm):
    B, S, _ = x.shape
    T = B * S
    nj = S // tm
    tok = lambda b, j: (0, b * nj + j)
    return pl.pallas_call(
        _outproj_router_kernel,
        grid=(B, nj),
        in_specs=[pl.BlockSpec((1, tm, D), lambda b, j: (b, j, 0)),
                  pl.BlockSpec((1, tm, D), lambda b, j: (b, j, 0)),
                  _resident((D, D)),
                  pl.BlockSpec((1, 1, D), lambda b, j: (b, 0, 0)),
                  pl.BlockSpec((1, 1, D), lambda b, j: (b, 0, 0)),
                  pl.BlockSpec((1, 1, D), lambda b, j: (b, 0, 0)),
                  _resident((N_EXP, D)), _resident((N_EXP, 1))],
        out_specs=[pl.BlockSpec((1, tm, D), lambda b, j: (b, j, 0)),
                   pl.BlockSpec((tm, PACK_S, 128), lambda b, j: (b * nj + j, 0, 0)),
                   pl.BlockSpec((TOP_K, tm), tok),
                   pl.BlockSpec((TOP_K, tm), tok),
                   pl.BlockSpec((TOP_K, tm), tok),
                   pl.BlockSpec((N_EXP, 128), lambda b, j: (0, 0))],
        out_shape=[jax.ShapeDtypeStruct((B, S, D), f32),
                   jax.ShapeDtypeStruct((T, PACK_S, 128), u32),
                   jax.ShapeDtypeStruct((TOP_K, T), i32),
                   jax.ShapeDtypeStruct((TOP_K, T), f32),
                   jax.ShapeDtypeStruct((TOP_K, T), i32),
                   jax.ShapeDtypeStruct((N_EXP, 128), f32)],
        scratch_shapes=[pltpu.VMEM((N_EXP, 128), f32)],
        compiler_params=_params(2),
        name="out_proj_router",
    )(x, y, w_out_b, g1, sc2, sh2, wrt, rb)


def _dest_kernel(pstart_ref, idx_ref, rank_ref, dest_ref):
    idx = idx_ref[...]
    acc = rank_ref[...]
    for e in range(N_EXP):
        acc = acc + jnp.where(idx == e, pstart_ref[e], 0)
    dest_ref[...] = acc


def _dest_call(pad_starts, idx, rank):
    T = idx.shape[1]
    tn = min(T, 8192)
    return pl.pallas_call(
        _dest_kernel,
        grid_spec=pltpu.PrefetchScalarGridSpec(
            num_scalar_prefetch=1,
            grid=(T // tn,),
            in_specs=[pl.BlockSpec((TOP_K, tn), lambda i, ps: (0, i)),
                      pl.BlockSpec((TOP_K, tn), lambda i, ps: (0, i))],
            out_specs=pl.BlockSpec((TOP_K, tn), lambda i, ps: (0, i))),
        out_shape=jax.ShapeDtypeStruct((TOP_K, T), i32),
        compiler_params=_params(1),
        name="dest_rows",
    )(pad_starts, idx, rank)


def _dispatch_kernel(dest_ref, h_ref, xs_in_ref, xs_ref, sem):
    del xs_in_ref
    tm = h_ref.shape[0]

    def issue(i, carry):
        for k in range(TOP_K):
            pltpu.make_async_copy(h_ref.at[i], xs_ref.at[dest_ref[0, TOP_K * i + k]], sem).start()
        return carry

    def drain(i, carry):
        for k in range(TOP_K):
            pltpu.make_async_copy(h_ref.at[0], xs_ref.at[0], sem).wait()
        return carry

    lax.fori_loop(0, tm, issue, 0)
    lax.fori_loop(0, tm, drain, 0)


def _dispatch_call(dest_tiles, hp, xs_zero, tm):
    T = hp.shape[0]
    rows = xs_zero.shape[0]
    return pl.pallas_call(
        _dispatch_kernel,
        grid=(T // tm,),
        in_specs=[pl.BlockSpec((None, 1, TOP_K * tm), lambda i: (i, 0, 0), memory_space=pltpu.SMEM),
                  pl.BlockSpec((tm, PACK_S, 128), lambda i: (i, 0, 0)),
                  pl.BlockSpec(memory_space=pl.ANY)],
        out_specs=pl.BlockSpec(memory_space=pl.ANY),
        out_shape=jax.ShapeDtypeStruct((rows, PACK_S, 128), u32),
        scratch_shapes=[pltpu.SemaphoreType.DMA],
        input_output_aliases={2: 0},
        compiler_params=_params(1),
        name="dispatch",
    )(dest_tiles, hp, xs_zero)


def _expert_kernel(be_ref, nu_ref, xs_ref, w1_ref, b1_ref, w2_ref, b2_ref, ys_ref, xbuf):
    del be_ref
    blk = pl.program_id(0)

    @pl.when(blk < nu_ref[0])
    def _():
        for s in range(PACK_S):
            lo, hi = _unpack_pair(xs_ref[:, s, :])
            xbuf[:, 256 * s:256 * s + 128] = lo.astype(bf16)
            xbuf[:, 256 * s + 128:256 * s + 256] = hi.astype(bf16)
        z = jnp.dot(xbuf[...], w1_ref[0], preferred_element_type=f32) + b1_ref[0]
        glu = jnp.minimum(z[:, 0:F_EXP], SWIGLU_LIMIT)
        lin = jnp.clip(z[:, F_EXP:], -SWIGLU_LIMIT, SWIGLU_LIMIT)
        a = glu * _sigmoid(SWIGLU_ALPHA * glu) * (lin + 1.0)
        o = jnp.dot(a.astype(bf16), w2_ref[0], preferred_element_type=f32) + b2_ref[0]
        for s in range(PACK_S):
            ys_ref[:, s, :] = _pack_pair(o[:, 256 * s:256 * s + 128], o[:, 256 * s + 128:256 * s + 256])

    @pl.when(blk >= nu_ref[0])
    def _():
        ys_ref[...] = jnp.zeros_like(ys_ref)


def _expert_call(block_e, n_used, xs, w1_b, b1, w2_b, b2, tm):
    rows = xs.shape[0]
    n_blocks = rows // tm
    used = lambda i, be, nu: jnp.minimum(i, jnp.maximum(nu[0] - 1, 0))
    return pl.pallas_call(
        _expert_kernel,
        grid_spec=pltpu.PrefetchScalarGridSpec(
            num_scalar_prefetch=2,
            grid=(n_blocks,),
            in_specs=[pl.BlockSpec((tm, PACK_S, 128), lambda i, be, nu: (used(i, be, nu), 0, 0)),
                      pl.BlockSpec((1, D, 2 * F_EXP), lambda i, be, nu: (be[i], 0, 0)),
                      pl.BlockSpec((1, 1, 2 * F_EXP), lambda i, be, nu: (be[i], 0, 0)),
                      pl.BlockSpec((1, F_EXP, D), lambda i, be, nu: (be[i], 0, 0)),
                      pl.BlockSpec((1, 1, D), lambda i, be, nu: (be[i], 0, 0))],
            out_specs=pl.BlockSpec((tm, PACK_S, 128), lambda i, be, nu: (i, 0, 0)),
            scratch_shapes=[pltpu.VMEM((tm, D), bf16)]),
        out_shape=jax.ShapeDtypeStruct((rows, PACK_S, 128), u32),
        compiler_params=_params(1),
        name="expert_ffn",
    )(block_e, n_used, xs, w1_b, b1, w2_b, b2)


def _combine_kernel(dest_ref, x_ref, gate_ref, g2_ref, fg_ref, ys_ref, o_ref,
                    gb0, gb1, gb2, gb3, fbuf, sem, *, final):
    tm = x_ref.shape[1]
    gbufs = (gb0, gb1, gb2, gb3)

    def issue(i, carry):
        for k in range(TOP_K):
            pltpu.make_async_copy(ys_ref.at[dest_ref[0, TOP_K * i + k]], gbufs[k].at[i], sem).start()
        return carry

    def drain(i, carry):
        for k in range(TOP_K):
            pltpu.make_async_copy(ys_ref.at[0], gbufs[k].at[0], sem).wait()
        return carry

    lax.fori_loop(0, tm, issue, 0)
    lax.fori_loop(0, tm, drain, 0)

    gates = [gate_ref[:, k:k + 1] for k in range(TOP_K)]
    for s in range(PACK_S):
        acc_lo = jnp.zeros((tm, 128), f32)
        acc_hi = jnp.zeros((tm, 128), f32)
        for k in range(TOP_K):
            lo, hi = _unpack_pair(gbufs[k][:, s, :])
            acc_lo = acc_lo + gates[k] * lo
            acc_hi = acc_hi + gates[k] * hi
        fbuf[:, 256 * s:256 * s + 128] = acc_lo
        fbuf[:, 256 * s + 128:256 * s + 256] = acc_hi
    out = x_ref[0] + g2_ref[0] * fbuf[...]
    if final:
        out = _rms(out) * fg_ref[...]
    o_ref[0] = out


def _combine_call(dest_tiles, x, gates_t, g2, final_g, ys, tm, final):
    B, S, _ = x.shape
    nj = S // tm
    return pl.pallas_call(
        functools.partial(_combine_kernel, final=final),
        grid=(B, nj),
        in_specs=[pl.BlockSpec((None, 1, TOP_K * tm), lambda b, j: (b * nj + j, 0, 0),
                               memory_space=pltpu.SMEM),
                  pl.BlockSpec((1, tm, D), lambda b, j: (b, j, 0)),
                  pl.BlockSpec((tm, TOP_K), lambda b, j: (b * nj + j, 0)),
                  pl.BlockSpec((1, 1, D), lambda b, j: (b, 0, 0)),
                  pl.BlockSpec((1, D), lambda b, j: (0, 0)),
                  pl.BlockSpec(memory_space=pl.ANY)],
        out_specs=pl.BlockSpec((1, tm, D), lambda b, j: (b, j, 0)),
        out_shape=jax.ShapeDtypeStruct((B, S, D), f32),
        scratch_shapes=[pltpu.VMEM((tm, PACK_S, 128), u32)] * TOP_K
        + [pltpu.VMEM((tm, D), f32), pltpu.SemaphoreType.DMA],
        compiler_params=_params(2),
        name="combine",
    )(dest_tiles, x, gates_t, g2, final_g, ys)


def _tile(n, want):
    t = min(n, want)
    assert n % t == 0 and t % CHUNK == 0, (n, t)
    return t


def kernel(x, c, ada_w, ada_b, w_in, conv_dw_w, conv_dw_b, conv_ln_g, conv_ln_b, conv_pw_w,
           pool_w, pool_scale, sg_ln_g, sg_ln_b, sg_w, sg_b, mix_norm_g, w_out,
           router_w, router_b, moe_w1, moe_b1, moe_w2, moe_b2, final_g):
    B, S, d_model = x.shape
    L = ada_w.shape[0]
    assert d_model == D and w_in.shape[-1] == IN_COLS and moe_w1.shape[1] == N_EXP
    T = B * S
    tm_in = _tile(S, 512)
    tm_mix = _tile(S, 256)
    tm_out = _tile(S, 512)
    tm_disp = _tile(S, 512)
    tm_exp = _tile(S, 512)
    tm_comb = _tile(S, 256)
    n_blocks = -(-(T * TOP_K) // tm_exp) + N_EXP
    rows = n_blocks * tm_exp

    mod = _ada_call(c, ada_w, ada_b)
    row = lambda a: a.reshape(1, -1)
    for l in range(L):
        sh1, sc1, g1, sh2, sc2, g2 = [m.reshape(B, 1, D) for m in jnp.split(mod[l], 6, axis=-1)]

        z = _inproj_call(x, sc1, sh1, w_in[l].astype(bf16), tm_in)
        y = _mixer_call(z, conv_dw_w[l], row(conv_dw_b[l]), row(conv_ln_g[l]), row(conv_ln_b[l]),
                        conv_pw_w[l].astype(bf16), pool_w[l].astype(bf16), row(pool_scale[l]),
                        row(sg_ln_g[l]), row(sg_ln_b[l]), sg_w[l], sg_b[l].T, row(mix_norm_g[l]),
                        tm_mix)
        x, hp, idx, gates, rank, cnt = _outproj_router_call(
            x, y, w_out[l].astype(bf16), g1, sc2, sh2, router_w[l].T,
            router_b[l].reshape(N_EXP, 1), tm_out)

        counts = cnt[:, 0].astype(i32)
        padded = (counts + tm_exp - 1) // tm_exp * tm_exp
        pad_ends = jnp.cumsum(padded)
        pad_starts = pad_ends - padded
        n_used = (pad_ends[-1:] // tm_exp).astype(i32)
        block_e = jnp.minimum(
            jnp.searchsorted(pad_ends, jnp.arange(n_blocks, dtype=i32) * tm_exp, side='right'),
            N_EXP - 1).astype(i32)
        dest = _dest_call(pad_starts.astype(i32), idx, rank)
        dest_tok = dest.T

        xs = _dispatch_call(dest_tok.reshape(T // tm_disp, 1, TOP_K * tm_disp), hp,
                            jnp.zeros((rows, PACK_S, 128), u32), tm_disp)
        ys = _expert_call(block_e, n_used, xs, moe_w1[l].astype(bf16),
                          moe_b1[l].reshape(N_EXP, 1, 2 * F_EXP), moe_w2[l].astype(bf16),
                          moe_b2[l].reshape(N_EXP, 1, D), tm_exp)
        x = _combine_call(dest_tok.reshape(T // tm_comb, 1, TOP_K * tm_comb), x, gates.T, g2,
                          row(final_g), ys, tm_comb, final=(l == L - 1))
    return x
```

```python
import functools

import jax
import jax.numpy as jnp
from jax import lax
from jax.experimental import pallas as pl
from jax.experimental.pallas import tpu as pltpu

f32 = jnp.float32
bf16 = jnp.bfloat16
u32 = jnp.uint32
i32 = jnp.int32

D = 2048
HEAD = 128
CONV_CH = 768
POOL_CH = 512
GATE_CH = 768
IN_COLS = 2 * CONV_CH + POOL_CH + 2 * GATE_CH
POOL_OFF = 2 * CONV_CH
GATE_OFF = 2 * CONV_CH + POOL_CH
POOL_WINDOWS = (2, 4, 8, 16)
CONV_W = 31
CHUNK = 128
N_EXP = 32
TOP_K = 4
F_EXP = 768
SWIGLU_ALPHA = 1.702
SWIGLU_LIMIT = 7.0
EPS = 1e-5

SUBLANES = 8
HALO = 32
CONV_ROWS = 64
PACK_W = D // 2
PACK_S = PACK_W // 128
assert PACK_S == SUBLANES
VMEM_LIMIT = 56 * 1024 * 1024


def _sigmoid(v):
    return 1.0 / (1.0 + jnp.exp(-v))


def _rms(v):
    return v * lax.rsqrt(jnp.mean(v * v, axis=-1, keepdims=True) + EPS)


def _layernorm(v, g, b):
    mu = jnp.mean(v, axis=-1, keepdims=True)
    d = v - mu
    var = jnp.mean(d * d, axis=-1, keepdims=True)
    return d * lax.rsqrt(var + EPS) * g + b


def _pack_pair(lo, hi):
    lo_bits = lax.bitcast_convert_type(lo.astype(bf16).astype(f32), u32)
    hi_bits = lax.bitcast_convert_type(hi.astype(bf16).astype(f32), u32)
    return (lo_bits >> 16) | (hi_bits & jnp.uint32(0xFFFF0000))


def _unpack_pair(w):
    lo = lax.bitcast_convert_type(w << 16, f32)
    hi = lax.bitcast_convert_type(w & jnp.uint32(0xFFFF0000), f32)
    return lo, hi


def _slab(n_tokens, s):
    return pl.ds(s, n_tokens, stride=PACK_S)


def _params(n_axes):
    return pltpu.CompilerParams(dimension_semantics=("arbitrary",) * n_axes,
                                vmem_limit_bytes=VMEM_LIMIT)


def _layer_resident(l, shape):
    nd = len(shape)
    return pl.BlockSpec((None,) + tuple(shape), lambda *_: (l,) + (0,) * nd,
                        pipeline_mode=pl.Buffered(1))


def _ada_kernel(c_ref, w_ref, b_ref, o_ref):
    c = c_ref[...]
    ca = (c * _sigmoid(c)).astype(bf16)
    o_ref[0] = jnp.dot(ca, w_ref[0].astype(bf16), preferred_element_type=f32) + b_ref[0]


def _ada_call(c, ada_w, ada_b):
    L, _, n6 = ada_w.shape
    B = c.shape[0]
    tn = 1024
    return pl.pallas_call(
        _ada_kernel,
        grid=(L, n6 // tn),
        in_specs=[pl.BlockSpec((B, D), lambda l, n: (0, 0)),
                  pl.BlockSpec((1, D, tn), lambda l, n: (l, 0, n)),
                  pl.BlockSpec((1, 1, tn), lambda l, n: (l, 0, n))],
        out_specs=pl.BlockSpec((1, B, tn), lambda l, n: (l, 0, n)),
        out_shape=jax.ShapeDtypeStruct((L, B, n6), f32),
        compiler_params=_params(2),
        name="ada_mod",
    )(c, ada_w, ada_b.reshape(L, 1, n6))


def _inproj_kernel(x_ref, sc_ref, sh_ref, w_ref, z_ref):
    h = _rms(x_ref[0]) * (1.0 + sc_ref[0]) + sh_ref[0]
    z_ref[0] = jnp.dot(h.astype(bf16), w_ref[...], preferred_element_type=f32).astype(bf16)


def _inproj_call(l, x, sc, sh, w_in_b, tm):
    B, S, _ = x.shape
    return pl.pallas_call(
        _inproj_kernel,
        grid=(B, S // tm),
        in_specs=[pl.BlockSpec((1, tm, D), lambda b, j: (b, j, 0)),
                  pl.BlockSpec((1, 1, D), lambda b, j: (b, 0, 0)),
                  pl.BlockSpec((1, 1, D), lambda b, j: (b, 0, 0)),
                  _layer_resident(l, (D, IN_COLS))],
        out_specs=pl.BlockSpec((1, tm, IN_COLS), lambda b, j: (b, j, 0)),
        out_shape=jax.ShapeDtypeStruct((B, S, IN_COLS), bf16),
        compiler_params=_params(2),
        name="in_proj",
    )(x, sc, sh, w_in_b)


def _mixer_kernel(z_ref, dww_ref, dwb_ref, clg_ref, clb_ref, pw_ref, poolw_ref, pscale_ref,
                  slg_ref, slb_ref, sgw_ref, sgbt_ref, mixg_ref, y_ref, abuf, pbuf, cbuf, shbuf):
    j = pl.program_id(1)
    tm = y_ref.shape[1]

    @pl.when(j == 0)
    def _():
        abuf[0:HALO, :] = jnp.zeros((HALO, CONV_CH), f32)
        pbuf[0:HALO, :] = jnp.zeros((HALO, POOL_CH), f32)

    za = z_ref[0, :, 0:CONV_CH].astype(f32)
    zg = z_ref[0, :, CONV_CH:2 * CONV_CH].astype(f32)
    abuf[HALO:HALO + tm, :] = za * _sigmoid(zg)
    first = HALO - (CONV_W - 1)
    span = CONV_ROWS + HALO
    for r0 in range(0, tm, CONV_ROWS):
        for c0 in range(0, CONV_CH, 128):
            win = abuf[r0:r0 + span, c0:c0 + 128]
            for res in range(1, SUBLANES):
                shbuf[res - 1] = win[res:res + span - SUBLANES, :]
            acc = jnp.broadcast_to(dwb_ref[:, c0:c0 + 128], (CONV_ROWS, 128))
            for t in range(CONV_W):
                q, res = divmod(first + t, SUBLANES)
                rows = slice(SUBLANES * q, SUBLANES * q + CONV_ROWS)
                src = win[rows, :] if res == 0 else shbuf[res - 1, rows, :]
                acc = acc + dww_ref[t:t + 1, c0:c0 + 128] * src
            cbuf[r0:r0 + CONV_ROWS, c0:c0 + 128] = acc
    abuf[0:HALO, :] = abuf[tm:tm + HALO, :]
    an = _layernorm(cbuf[...], clg_ref[...], clb_ref[...])
    an = an * _sigmoid(an)
    ya = _rms(jnp.dot(an.astype(bf16), pw_ref[...], preferred_element_type=f32))
    y_ref[0, :, 0:CONV_CH] = (ya * mixg_ref[:, 0:CONV_CH]).astype(bf16)

    p = z_ref[0, :, POOL_OFF:POOL_OFF + POOL_CH].astype(f32)
    pbuf[HALO:HALO + tm, :] = p
    pos = (j * tm + 1 + lax.broadcasted_iota(i32, (tm, 1), 0)).astype(f32)
    ybs = []
    for g, w in enumerate(POOL_WINDOWS):
        c0 = g * HEAD
        win = pbuf[HALO:HALO + tm, c0:c0 + HEAD]
        for i in range(1, w):
            win = win + pbuf[HALO - i:HALO - i + tm, c0:c0 + HEAD]
        diff = win / jnp.minimum(pos, float(w)) - p[:, c0:c0 + HEAD]
        yg = jnp.dot(diff.astype(bf16), poolw_ref[g], preferred_element_type=f32)
        ybs.append(yg * pscale_ref[:, c0:c0 + HEAD])
    pbuf[0:HALO, :] = pbuf[tm:tm + HALO, :]
    yb = _rms(jnp.concatenate(ybs, axis=-1))
    y_ref[0, :, CONV_CH:CONV_CH + POOL_CH] = (yb * mixg_ref[:, CONV_CH:CONV_CH + POOL_CH]).astype(bf16)

    zc = z_ref[0, :, GATE_OFF:GATE_OFF + 2 * GATE_CH].astype(f32)
    gl = 0.5 * zc * (1.0 + jnp.tanh(0.7978845608028654 * (zc + 0.044715 * zc * zc * zc)))
    u = gl[:, 0:GATE_CH]
    vn = _layernorm(gl[:, GATE_CH:], slg_ref[...], slb_ref[...]).astype(bf16)
    causal = (lax.broadcasted_iota(i32, (CHUNK, CHUNK), 1)
              <= lax.broadcasted_iota(i32, (CHUNK, CHUNK), 0))
    for hh in range(GATE_CH // HEAD):
        c0 = hh * HEAD
        wm = jnp.where(causal, sgw_ref[hh], 0.0).astype(bf16)
        bias = sgbt_ref[:, hh:hh + 1]
        for n in range(tm // CHUNK):
            r0 = n * CHUNK
            s = jnp.dot(wm, vn[r0:r0 + CHUNK, c0:c0 + HEAD], preferred_element_type=f32) + bias
            cbuf[r0:r0 + CHUNK, c0:c0 + HEAD] = u[r0:r0 + CHUNK, c0:c0 + HEAD] * s
    yc = _rms(cbuf[...])
    y_ref[0, :, CONV_CH + POOL_CH:] = (yc * mixg_ref[:, CONV_CH + POOL_CH:]).astype(bf16)


def _mixer_call(l, z, dww, dwb, clg, clb, pw_b, poolw_b, pscale, slg, slb, sgw, sgbt, mixg, tm):
    B, S, _ = z.shape
    n_heads = GATE_CH // HEAD
    lr = functools.partial(_layer_resident, l)
    return pl.pallas_call(
        _mixer_kernel,
        grid=(B, S // tm),
        in_specs=[pl.BlockSpec((1, tm, IN_COLS), lambda b, j: (b, j, 0)),
                  lr((CONV_W, CONV_CH)), lr((1, CONV_CH)), lr((1, CONV_CH)), lr((1, CONV_CH)),
                  lr((CONV_CH, CONV_CH)),
                  lr((len(POOL_WINDOWS), HEAD, HEAD)), lr((1, POOL_CH)),
                  lr((1, GATE_CH)), lr((1, GATE_CH)),
                  lr((n_heads, CHUNK, CHUNK)), lr((CHUNK, n_heads)),
                  lr((1, D))],
        out_specs=pl.BlockSpec((1, tm, D), lambda b, j: (b, j, 0)),
        out_shape=jax.ShapeDtypeStruct((B, S, D), bf16),
        scratch_shapes=[pltpu.VMEM((HALO + tm, CONV_CH), f32),
                        pltpu.VMEM((HALO + tm, POOL_CH), f32),
                        pltpu.VMEM((tm, CONV_CH), f32),
                        pltpu.VMEM((SUBLANES - 1, CONV_ROWS + HALO - SUBLANES, 128), f32)],
        compiler_params=_params(2),
        name="mixers",
    )(z, dww, dwb, clg, clb, pw_b, poolw_b, pscale, slg, slb, sgw, sgbt, mixg)


def _outproj_router_kernel(x_ref, y_ref, wout_ref, g1_ref, sc_ref, sh_ref, wrt_ref, rb_ref,
                           xo_ref, hp_ref, idx_ref, gate_ref, rank_ref, cnt_ref, carry):
    tm = x_ref.shape[1]

    @pl.when((pl.program_id(0) == 0) & (pl.program_id(1) == 0))
    def _():
        carry[...] = jnp.zeros_like(carry)

    x = x_ref[0] + g1_ref[0] * jnp.dot(y_ref[0], wout_ref[...], preferred_element_type=f32)
    xo_ref[0] = x
    h = _rms(x) * (1.0 + sc_ref[0]) + sh_ref[0]
    h_hi = h.astype(bf16)
    h_hi32 = h_hi.astype(f32)
    h_lo = (h - h_hi32).astype(bf16)
    for s in range(PACK_S):
        hp_ref[_slab(tm, s), :] = _pack_pair(h_hi32[:, 256 * s:256 * s + 128],
                                             h_hi32[:, 256 * s + 128:256 * s + 256])

    wr = wrt_ref[...]
    w_hi = wr.astype(bf16)
    w_lo = (wr - w_hi.astype(f32)).astype(bf16)
    nt = (((1,), (1,)), ((), ()))
    lg = (lax.dot_general(w_hi, h_hi, nt, preferred_element_type=f32)
          + lax.dot_general(w_hi, h_lo, nt, preferred_element_type=f32)
          + lax.dot_general(w_lo, h_hi, nt, preferred_element_type=f32)
          + rb_ref[...])

    e_iota = lax.broadcasted_iota(i32, (N_EXP, tm), 0)
    vals, idxs, hots = [], [], []
    for _ in range(TOP_K):
        m = jnp.max(lg, axis=0, keepdims=True)
        idx = jnp.min(jnp.where(lg == m, e_iota, N_EXP), axis=0, keepdims=True)
        hot = e_iota == idx
        lg = jnp.where(hot, -jnp.inf, lg)
        vals.append(m)
        idxs.append(idx)
        hots.append(hot)
    exps = [jnp.exp(v - vals[0]) for v in vals]
    denom = exps[0] + exps[1] + exps[2] + exps[3]
    idx_ref[...] = jnp.concatenate(idxs, axis=0)
    gate_ref[...] = jnp.concatenate([e / denom for e in exps], axis=0)

    sel = jnp.zeros((N_EXP, tm), f32)
    for hot in hots:
        sel = sel + jnp.where(hot, 1.0, 0.0)
    upper = jnp.where(lax.broadcasted_iota(i32, (tm, tm), 0) <= lax.broadcasted_iota(i32, (tm, tm), 1),
                      1.0, 0.0).astype(bf16)
    cum = jnp.dot(sel.astype(bf16), upper, preferred_element_type=f32)
    before = cum - sel + carry[:, 0:1]
    ranks = [jnp.sum(jnp.where(hot, before, 0.0), axis=0, keepdims=True) for hot in hots]
    rank_ref[...] = jnp.concatenate(ranks, axis=0).astype(i32)
    carry[...] = carry[...] + cum[:, tm - 1:tm]
    cnt_ref[...] = carry[...]


def _outproj_router_call(l, x, y, w_out_b, g1, sc2, sh2, wrt, rb, tm):
    B, S, _ = x.shape
    T = B * S
    nj = S // tm
    tok = lambda b, j: (0, b * nj + j)
    return pl.pallas_call(
        _outproj_router_kernel,
        grid=(B, nj),
        in_specs=[pl.BlockSpec((1, tm, D), lambda b, j: (b, j, 0)),
                  pl.BlockSpec((1, tm, D), lambda b, j: (b, j, 0)),
                  _layer_resident(l, (D, D)),
                  pl.BlockSpec((1, 1, D), lambda b, j: (b, 0, 0)),
                  pl.BlockSpec((1, 1, D), lambda b, j: (b, 0, 0)),
                  pl.BlockSpec((1, 1, D), lambda b, j: (b, 0, 0)),
                  _layer_resident(l, (N_EXP, D)), _layer_resident(l, (N_EXP, 1))],
        out_specs=[pl.BlockSpec((1, tm, D), lambda b, j: (b, j, 0)),
                   pl.BlockSpec((tm * PACK_S, 128), lambda b, j: (b * nj + j, 0)),
                   pl.BlockSpec((TOP_K, tm), tok),
                   pl.BlockSpec((TOP_K, tm), tok),
                   pl.BlockSpec((TOP_K, tm), tok),
                   pl.BlockSpec((N_EXP, 128), lambda b, j: (0, 0))],
        out_shape=[jax.ShapeDtypeStruct((B, S, D), f32),
                   jax.ShapeDtypeStruct((T * PACK_S, 128), u32),
                   jax.ShapeDtypeStruct((TOP_K, T), i32),
                   jax.ShapeDtypeStruct((TOP_K, T), f32),
                   jax.ShapeDtypeStruct((TOP_K, T), i32),
                   jax.ShapeDtypeStruct((N_EXP, 128), f32)],
        scratch_shapes=[pltpu.VMEM((N_EXP, 128), f32)],
        compiler_params=_params(2),
        name="out_proj_router",
    )(x, y, w_out_b, g1, sc2, sh2, wrt, rb)


def _dest_kernel(pstart_ref, idx_ref, rank_ref, dest_ref):
    idx = idx_ref[...]
    acc = rank_ref[...]
    for e in range(N_EXP):
        acc = acc + jnp.where(idx == e, pstart_ref[e], 0)
    dest_ref[...] = acc * PACK_S


def _dest_call(pad_starts, idx, rank):
    T = idx.shape[1]
    tn = min(T, 8192)
    return pl.pallas_call(
        _dest_kernel,
        grid_spec=pltpu.PrefetchScalarGridSpec(
            num_scalar_prefetch=1,
            grid=(T // tn,),
            in_specs=[pl.BlockSpec((TOP_K, tn), lambda i, ps: (0, i)),
                      pl.BlockSpec((TOP_K, tn), lambda i, ps: (0, i))],
            out_specs=pl.BlockSpec((TOP_K, tn), lambda i, ps: (0, i))),
        out_shape=jax.ShapeDtypeStruct((TOP_K, T), i32),
        compiler_params=_params(1),
        name="dest_rows",
    )(pad_starts, idx, rank)


def _dispatch_kernel(fill_lo_ref, fill_hi_ref, dest_ref, h_ref, xs_ref, zbuf, sem, fill_sem):
    tm = h_ref.shape[0] // PACK_S

    def row(off):
        return pl.ds(pl.multiple_of(off, PACK_S), PACK_S)

    @pl.when(pl.program_id(0) == 0)
    def _():
        zbuf[...] = jnp.zeros_like(zbuf)
        for e in range(N_EXP):
            lo = fill_lo_ref[e]
            n_fill = fill_hi_ref[e] - lo

            def fill(r, carry):
                pltpu.make_async_copy(zbuf, xs_ref.at[row((lo + r) * PACK_S)], fill_sem).start()
                return carry

            def fill_wait(r, carry):
                pltpu.make_async_copy(zbuf, xs_ref.at[row(0)], fill_sem).wait()
                return carry

            lax.fori_loop(0, n_fill, fill, 0)
            lax.fori_loop(0, n_fill, fill_wait, 0)

    def issue(i, carry):
        for k in range(TOP_K):
            pltpu.make_async_copy(h_ref.at[row(i * PACK_S)],
                                  xs_ref.at[row(dest_ref[0, TOP_K * i + k])], sem).start()
        return carry

    def drain(i, carry):
        for k in range(TOP_K):
            pltpu.make_async_copy(h_ref.at[row(0)], xs_ref.at[row(0)], sem).wait()
        return carry

    lax.fori_loop(0, tm, issue, 0)
    lax.fori_loop(0, tm, drain, 0)


def _dispatch_call(fill_lo, fill_hi, dest_tiles, hp, rows, tm):
    T = hp.shape[0] // PACK_S
    return pl.pallas_call(
        _dispatch_kernel,
        grid_spec=pltpu.PrefetchScalarGridSpec(
            num_scalar_prefetch=2,
            grid=(T // tm,),
            in_specs=[pl.BlockSpec((None, 1, TOP_K * tm), lambda i, lo, hi: (i, 0, 0),
                                   memory_space=pltpu.SMEM),
                      pl.BlockSpec((tm * PACK_S, 128), lambda i, lo, hi: (i, 0))],
            out_specs=pl.BlockSpec(memory_space=pl.ANY),
            scratch_shapes=[pltpu.VMEM((PACK_S, 128), u32),
                            pltpu.SemaphoreType.DMA, pltpu.SemaphoreType.DMA]),
        out_shape=jax.ShapeDtypeStruct((rows * PACK_S, 128), u32),
        compiler_params=_params(1),
        name="dispatch",
    )(fill_lo, fill_hi, dest_tiles, hp)


def _expert_kernel(be_ref, nu_ref, xs_ref, w1_ref, b1_ref, w2_ref, b2_ref, ys_ref, xbuf):
    del be_ref
    blk = pl.program_id(0)
    tm = xbuf.shape[0]

    @pl.when(blk < nu_ref[0])
    def _():
        for s in range(PACK_S):
            lo, hi = _unpack_pair(xs_ref[_slab(tm, s), :])
            xbuf[:, 256 * s:256 * s + 128] = lo.astype(bf16)
            xbuf[:, 256 * s + 128:256 * s + 256] = hi.astype(bf16)
        z = jnp.dot(xbuf[...], w1_ref[...], preferred_element_type=f32) + b1_ref[...]
        glu = jnp.minimum(z[:, 0:F_EXP], SWIGLU_LIMIT)
        lin = jnp.clip(z[:, F_EXP:], -SWIGLU_LIMIT, SWIGLU_LIMIT)
        a = glu * _sigmoid(SWIGLU_ALPHA * glu) * (lin + 1.0)
        o = jnp.dot(a.astype(bf16), w2_ref[...], preferred_element_type=f32) + b2_ref[...]
        for s in range(PACK_S):
            ys_ref[_slab(tm, s), :] = _pack_pair(o[:, 256 * s:256 * s + 128],
                                                 o[:, 256 * s + 128:256 * s + 256])

    @pl.when(blk >= nu_ref[0])
    def _():
        ys_ref[...] = jnp.zeros_like(ys_ref)


def _expert_call(l, block_e, n_used, xs, w1_b, b1, w2_b, b2, tm):
    rows = xs.shape[0] // PACK_S
    n_blocks = rows // tm
    used = lambda i, be, nu: jnp.minimum(i, jnp.maximum(nu[0] - 1, 0))
    per_expert = lambda shape: pl.BlockSpec((None, None) + shape,
                                            lambda i, be, nu: (l, be[i]) + (0,) * len(shape))
    return pl.pallas_call(
        _expert_kernel,
        grid_spec=pltpu.PrefetchScalarGridSpec(
            num_scalar_prefetch=2,
            grid=(n_blocks,),
            in_specs=[pl.BlockSpec((tm * PACK_S, 128), lambda i, be, nu: (used(i, be, nu), 0)),
                      per_expert((D, 2 * F_EXP)), per_expert((1, 2 * F_EXP)),
                      per_expert((F_EXP, D)), per_expert((1, D))],
            out_specs=pl.BlockSpec((tm * PACK_S, 128), lambda i, be, nu: (i, 0)),
            scratch_shapes=[pltpu.VMEM((tm, D), bf16)]),
        out_shape=jax.ShapeDtypeStruct((rows * PACK_S, 128), u32),
        compiler_params=_params(1),
        name="expert_ffn",
    )(block_e, n_used, xs, w1_b, b1, w2_b, b2)


def _combine_kernel(dest_ref, dest_next_ref, x_ref, gate_ref, g2_ref, fg_ref, ys_ref, o_ref,
                    gbuf, fbuf, sems, *, final):
    tm = x_ref.shape[1]
    step = pl.program_id(0) * pl.num_programs(1) + pl.program_id(1)
    n_steps = pl.num_programs(0) * pl.num_programs(1)

    def row(off):
        return pl.ds(pl.multiple_of(off, PACK_S), PACK_S)

    def issue(dref, slot):
        def body(i, carry):
            for k in range(TOP_K):
                pltpu.make_async_copy(ys_ref.at[row(dref[0, TOP_K * i + k])],
                                      gbuf.at[slot * TOP_K + k, row(i * PACK_S)],
                                      sems.at[slot]).start()
            return carry
        lax.fori_loop(0, tm, body, 0)

    def drain(slot):
        def body(i, carry):
            for k in range(TOP_K):
                pltpu.make_async_copy(ys_ref.at[row(0)], gbuf.at[slot * TOP_K + k, row(0)],
                                      sems.at[slot]).wait()
            return carry
        lax.fori_loop(0, tm, body, 0)

    def compute(slot):
        gates = [gate_ref[:, k:k + 1] for k in range(TOP_K)]
        for s in range(PACK_S):
            acc_lo = jnp.zeros((tm, 128), f32)
            acc_hi = jnp.zeros((tm, 128), f32)
            for k in range(TOP_K):
                lo, hi = _unpack_pair(gbuf[slot * TOP_K + k, _slab(tm, s), :])
                acc_lo = acc_lo + gates[k] * lo
                acc_hi = acc_hi + gates[k] * hi
            fbuf[:, 256 * s:256 * s + 128] = acc_lo
            fbuf[:, 256 * s + 128:256 * s + 256] = acc_hi
        out = x_ref[0] + g2_ref[0] * fbuf[...]
        if final:
            out = _rms(out) * fg_ref[...]
        o_ref[0] = out

    @pl.when(step == 0)
    def _():
        issue(dest_ref, 0)

    for slot in range(2):
        @pl.when((step + 1 < n_steps) & ((step + 1) % 2 == slot))
        def _():
            issue(dest_next_ref, slot)

    for slot in range(2):
        @pl.when(step % 2 == slot)
        def _():
            drain(slot)
            compute(slot)


def _combine_call(dest_tiles, x, gates_t, g2, final_g, ys, tm, final):
    B, S, _ = x.shape
    nj = S // tm
    n_tiles = B * nj
    dest_spec = lambda index: pl.BlockSpec((None, 1, TOP_K * tm), index, memory_space=pltpu.SMEM)
    return pl.pallas_call(
        functools.partial(_combine_kernel, final=final),
        grid=(B, nj),
        in_specs=[dest_spec(lambda b, j: (b * nj + j, 0, 0)),
                  dest_spec(lambda b, j: (jnp.minimum(b * nj + j + 1, n_tiles - 1), 0, 0)),
                  pl.BlockSpec((1, tm, D), lambda b, j: (b, j, 0)),
                  pl.BlockSpec((tm, TOP_K), lambda b, j: (b * nj + j, 0)),
                  pl.BlockSpec((1, 1, D), lambda b, j: (b, 0, 0)),
                  pl.BlockSpec((1, D), lambda b, j: (0, 0)),
                  pl.BlockSpec(memory_space=pl.ANY)],
        out_specs=pl.BlockSpec((1, tm, D), lambda b, j: (b, j, 0)),
        out_shape=jax.ShapeDtypeStruct((B, S, D), f32),
        scratch_shapes=[pltpu.VMEM((2 * TOP_K, tm * PACK_S, 128), u32),
                        pltpu.VMEM((tm, D), f32),
                        pltpu.SemaphoreType.DMA((2,))],
        compiler_params=_params(2),
        name="combine",
    )(dest_tiles, dest_tiles, x, gates_t, g2, final_g, ys)


def _tile(n, want):
    t = min(n, want)
    assert n % t == 0 and t % CHUNK == 0, (n, t)
    return t


def kernel(x, c, ada_w, ada_b, w_in, conv_dw_w, conv_dw_b, conv_ln_g, conv_ln_b, conv_pw_w,
           pool_w, pool_scale, sg_ln_g, sg_ln_b, sg_w, sg_b, mix_norm_g, w_out,
           router_w, router_b, moe_w1, moe_b1, moe_w2, moe_b2, final_g):
    B, S, d_model = x.shape
    L = ada_w.shape[0]
    assert d_model == D and w_in.shape[-1] == IN_COLS and moe_w1.shape[1] == N_EXP
    T = B * S
    tm_in = _tile(S, 512)
    tm_mix = _tile(S, 256)
    tm_out = _tile(S, 512)
    tm_disp = _tile(S, 512)
    tm_exp = _tile(S, 512)
    tm_comb = _tile(S, 256)
    n_blocks = -(-(T * TOP_K) // tm_exp) + N_EXP
    rows = n_blocks * tm_exp

    w_in_b, w_out_b, pw_b, poolw_b = (w.astype(bf16) for w in (w_in, w_out, conv_pw_w, pool_w))
    w1_b, w2_b = moe_w1.astype(bf16), moe_w2.astype(bf16)
    rows3 = lambda a: a.reshape(L, 1, -1)
    dwb, clg, clb, pscale, slg, slb, mixg = (
        rows3(a) for a in (conv_dw_b, conv_ln_g, conv_ln_b, pool_scale, sg_ln_g, sg_ln_b, mix_norm_g))
    sgbt = jnp.swapaxes(sg_b, 1, 2)
    wrt = jnp.swapaxes(router_w, 1, 2)
    rb = router_b.reshape(L, N_EXP, 1)
    b1 = moe_b1.reshape(L, N_EXP, 1, 2 * F_EXP)
    b2 = moe_b2.reshape(L, N_EXP, 1, D)
    fg = final_g.reshape(1, D)

    mod = _ada_call(c, ada_w, ada_b)
    for l in range(L):
        sh1, sc1, g1, sh2, sc2, g2 = [m.reshape(B, 1, D) for m in jnp.split(mod[l], 6, axis=-1)]

        z = _inproj_call(l, x, sc1, sh1, w_in_b, tm_in)
        y = _mixer_call(l, z, conv_dw_w, dwb, clg, clb, pw_b, poolw_b, pscale, slg, slb, sg_w,
                        sgbt, mixg, tm_mix)
        x, hp, idx, gates, rank, cnt = _outproj_router_call(
            l, x, y, w_out_b, g1, sc2, sh2, wrt, rb, tm_out)

        counts = cnt[:, 0].astype(i32)
        padded = (counts + tm_exp - 1) // tm_exp * tm_exp
        pad_ends = jnp.cumsum(padded)
        pad_starts = pad_ends - padded
        n_used = (pad_ends[-1:] // tm_exp).astype(i32)
        block_row0 = jnp.arange(n_blocks, dtype=i32) * tm_exp
        block_e = jnp.minimum(jnp.sum(block_row0[:, None] >= pad_ends[None, :], axis=1),
                              N_EXP - 1).astype(i32)
        dest = _dest_call(pad_starts.astype(i32), idx, rank)
        dest_tok = dest.T

        xs = _dispatch_call((pad_starts + counts).astype(i32), pad_ends.astype(i32),
                            dest_tok.reshape(T // tm_disp, 1, TOP_K * tm_disp), hp, rows, tm_disp)
        ys = _expert_call(l, block_e, n_used, xs, w1_b, b1, w2_b, b2, tm_exp)
        x = _combine_call(dest_tok.reshape(T // tm_comb, 1, TOP_K * tm_comb), x, gates.T, g2,
                          fg, ys, tm_comb, final=(l == L - 1))
    return x
```

```python
import functools

import jax
import jax.numpy as jnp
from jax import lax
from jax.experimental import pallas as pl
from jax.experimental.pallas import tpu as pltpu

f32 = jnp.float32
bf16 = jnp.bfloat16
u32 = jnp.uint32
i32 = jnp.int32

D = 2048
HEAD = 128
CONV_CH = 768
POOL_CH = 512
GATE_CH = 768
IN_COLS = 2 * CONV_CH + POOL_CH + 2 * GATE_CH
POOL_OFF = 2 * CONV_CH
GATE_OFF = 2 * CONV_CH + POOL_CH
POOL_WINDOWS = (2, 4, 8, 16)
CONV_W = 31
CHUNK = 128
N_EXP = 32
TOP_K = 4
F_EXP = 768
SWIGLU_ALPHA = 1.702
SWIGLU_LIMIT = 7.0
EPS = 1e-5

SUBLANES = 8
HALO = 32
CONV_ROWS = 64
PACK_W = D // 2
PACK_S = PACK_W // 128
assert PACK_S == SUBLANES
VMEM_LIMIT = 56 * 1024 * 1024


def _sigmoid(v):
    return 1.0 / (1.0 + jnp.exp(-v))


def _rms(v):
    return v * lax.rsqrt(jnp.mean(v * v, axis=-1, keepdims=True) + EPS)


def _layernorm(v, g, b):
    mu = jnp.mean(v, axis=-1, keepdims=True)
    d = v - mu
    var = jnp.mean(d * d, axis=-1, keepdims=True)
    return d * lax.rsqrt(var + EPS) * g + b


def _pack_pair(lo, hi):
    lo_bits = lax.bitcast_convert_type(lo.astype(bf16).astype(f32), u32)
    hi_bits = lax.bitcast_convert_type(hi.astype(bf16).astype(f32), u32)
    return (lo_bits >> 16) | (hi_bits & jnp.uint32(0xFFFF0000))


def _unpack_pair(w):
    lo = lax.bitcast_convert_type(w << 16, f32)
    hi = lax.bitcast_convert_type(w & jnp.uint32(0xFFFF0000), f32)
    return lo, hi


def _slab(n_tokens, s):
    return pl.ds(s, n_tokens, stride=PACK_S)


def _rows(first_row, n_rows):
    start = first_row * PACK_S
    if not isinstance(start, int):
        start = pl.multiple_of(start, PACK_S)
    return pl.ds(start, n_rows * PACK_S)


def _params(n_axes):
    return pltpu.CompilerParams(dimension_semantics=("arbitrary",) * n_axes,
                                vmem_limit_bytes=VMEM_LIMIT)


def _layer_resident(l, shape):
    nd = len(shape)
    return pl.BlockSpec((None,) + tuple(shape), lambda *_: (l,) + (0,) * nd,
                        pipeline_mode=pl.Buffered(1))


def _ada_kernel(c_ref, w_ref, b_ref, o_ref):
    c = c_ref[...]
    ca = (c * _sigmoid(c)).astype(bf16)
    o_ref[0] = jnp.dot(ca, w_ref[0].astype(bf16), preferred_element_type=f32) + b_ref[0]


def _ada_call(c, ada_w, ada_b):
    L, _, n6 = ada_w.shape
    B = c.shape[0]
    tn = 1024
    return pl.pallas_call(
        _ada_kernel,
        grid=(L, n6 // tn),
        in_specs=[pl.BlockSpec((B, D), lambda l, n: (0, 0)),
                  pl.BlockSpec((1, D, tn), lambda l, n: (l, 0, n)),
                  pl.BlockSpec((1, 1, tn), lambda l, n: (l, 0, n))],
        out_specs=pl.BlockSpec((1, B, tn), lambda l, n: (l, 0, n)),
        out_shape=jax.ShapeDtypeStruct((L, B, n6), f32),
        compiler_params=_params(2),
        name="ada_mod",
    )(c, ada_w, ada_b.reshape(L, 1, n6))


def _inproj_kernel(x_ref, sc_ref, sh_ref, w_ref, z_ref):
    h = _rms(x_ref[0]) * (1.0 + sc_ref[0]) + sh_ref[0]
    z_ref[0] = jnp.dot(h.astype(bf16), w_ref[...], preferred_element_type=f32).astype(bf16)


def _inproj_call(l, x, sc, sh, w_in_b, tm):
    B, S, _ = x.shape
    return pl.pallas_call(
        _inproj_kernel,
        grid=(B, S // tm),
        in_specs=[pl.BlockSpec((1, tm, D), lambda b, j: (b, j, 0)),
                  pl.BlockSpec((1, 1, D), lambda b, j: (b, 0, 0)),
                  pl.BlockSpec((1, 1, D), lambda b, j: (b, 0, 0)),
                  _layer_resident(l, (D, IN_COLS))],
        out_specs=pl.BlockSpec((1, tm, IN_COLS), lambda b, j: (b, j, 0)),
        out_shape=jax.ShapeDtypeStruct((B, S, IN_COLS), bf16),
        compiler_params=_params(2),
        name="in_proj",
    )(x, sc, sh, w_in_b)


def _mixer_kernel(z_ref, dww_ref, dwb_ref, clg_ref, clb_ref, pw_ref, poolw_ref, pscale_ref,
                  slg_ref, slb_ref, sgw_ref, sgbt_ref, mixg_ref, y_ref, abuf, pbuf, cbuf, shbuf):
    j = pl.program_id(1)
    tm = y_ref.shape[1]

    @pl.when(j == 0)
    def _():
        abuf[0:HALO, :] = jnp.zeros((HALO, CONV_CH), f32)
        pbuf[0:HALO, :] = jnp.zeros((HALO, POOL_CH), f32)

    za = z_ref[0, :, 0:CONV_CH].astype(f32)
    zg = z_ref[0, :, CONV_CH:2 * CONV_CH].astype(f32)
    abuf[HALO:HALO + tm, :] = za * _sigmoid(zg)
    first = HALO - (CONV_W - 1)
    span = CONV_ROWS + HALO
    for r0 in range(0, tm, CONV_ROWS):
        for c0 in range(0, CONV_CH, 128):
            win = abuf[r0:r0 + span, c0:c0 + 128]
            for res in range(1, SUBLANES):
                shbuf[res - 1] = win[res:res + span - SUBLANES, :]
            acc = jnp.broadcast_to(dwb_ref[:, c0:c0 + 128], (CONV_ROWS, 128))
            for t in range(CONV_W):
                q, res = divmod(first + t, SUBLANES)
                rows = slice(SUBLANES * q, SUBLANES * q + CONV_ROWS)
                src = win[rows, :] if res == 0 else shbuf[res - 1, rows, :]
                acc = acc + dww_ref[t:t + 1, c0:c0 + 128] * src
            cbuf[r0:r0 + CONV_ROWS, c0:c0 + 128] = acc
    abuf[0:HALO, :] = abuf[tm:tm + HALO, :]
    an = _layernorm(cbuf[...], clg_ref[...], clb_ref[...])
    an = an * _sigmoid(an)
    ya = _rms(jnp.dot(an.astype(bf16), pw_ref[...], preferred_element_type=f32))
    y_ref[0, :, 0:CONV_CH] = (ya * mixg_ref[:, 0:CONV_CH]).astype(bf16)

    p = z_ref[0, :, POOL_OFF:POOL_OFF + POOL_CH].astype(f32)
    pbuf[HALO:HALO + tm, :] = p
    pos = (j * tm + 1 + lax.broadcasted_iota(i32, (tm, 1), 0)).astype(f32)
    ybs = []
    for g, w in enumerate(POOL_WINDOWS):
        c0 = g * HEAD
        win = pbuf[HALO:HALO + tm, c0:c0 + HEAD]
        for i in range(1, w):
            win = win + pbuf[HALO - i:HALO - i + tm, c0:c0 + HEAD]
        diff = win / jnp.minimum(pos, float(w)) - p[:, c0:c0 + HEAD]
        yg = jnp.dot(diff.astype(bf16), poolw_ref[g], preferred_element_type=f32)
        ybs.append(yg * pscale_ref[:, c0:c0 + HEAD])
    pbuf[0:HALO, :] = pbuf[tm:tm + HALO, :]
    yb = _rms(jnp.concatenate(ybs, axis=-1))
    y_ref[0, :, CONV_CH:CONV_CH + POOL_CH] = (yb * mixg_ref[:, CONV_CH:CONV_CH + POOL_CH]).astype(bf16)

    zc = z_ref[0, :, GATE_OFF:GATE_OFF + 2 * GATE_CH].astype(f32)
    gl = 0.5 * zc * (1.0 + jnp.tanh(0.7978845608028654 * (zc + 0.044715 * zc * zc * zc)))
    u = gl[:, 0:GATE_CH]
    vn = _layernorm(gl[:, GATE_CH:], slg_ref[...], slb_ref[...]).astype(bf16)
    causal = (lax.broadcasted_iota(i32, (CHUNK, CHUNK), 1)
              <= lax.broadcasted_iota(i32, (CHUNK, CHUNK), 0))
    for hh in range(GATE_CH // HEAD):
        c0 = hh * HEAD
        wm = jnp.where(causal, sgw_ref[hh], 0.0).astype(bf16)
        bias = sgbt_ref[:, hh:hh + 1]
        for n in range(tm // CHUNK):
            r0 = n * CHUNK
            s = jnp.dot(wm, vn[r0:r0 + CHUNK, c0:c0 + HEAD], preferred_element_type=f32) + bias
            cbuf[r0:r0 + CHUNK, c0:c0 + HEAD] = u[r0:r0 + CHUNK, c0:c0 + HEAD] * s
    yc = _rms(cbuf[...])
    y_ref[0, :, CONV_CH + POOL_CH:] = (yc * mixg_ref[:, CONV_CH + POOL_CH:]).astype(bf16)


def _mixer_call(l, z, dww, dwb, clg, clb, pw_b, poolw_b, pscale, slg, slb, sgw, sgbt, mixg, tm):
    B, S, _ = z.shape
    n_heads = GATE_CH // HEAD
    lr = functools.partial(_layer_resident, l)
    return pl.pallas_call(
        _mixer_kernel,
        grid=(B, S // tm),
        in_specs=[pl.BlockSpec((1, tm, IN_COLS), lambda b, j: (b, j, 0)),
                  lr((CONV_W, CONV_CH)), lr((1, CONV_CH)), lr((1, CONV_CH)), lr((1, CONV_CH)),
                  lr((CONV_CH, CONV_CH)),
                  lr((len(POOL_WINDOWS), HEAD, HEAD)), lr((1, POOL_CH)),
                  lr((1, GATE_CH)), lr((1, GATE_CH)),
                  lr((n_heads, CHUNK, CHUNK)), lr((CHUNK, n_heads)),
                  lr((1, D))],
        out_specs=pl.BlockSpec((1, tm, D), lambda b, j: (b, j, 0)),
        out_shape=jax.ShapeDtypeStruct((B, S, D), bf16),
        scratch_shapes=[pltpu.VMEM((HALO + tm, CONV_CH), f32),
                        pltpu.VMEM((HALO + tm, POOL_CH), f32),
                        pltpu.VMEM((tm, CONV_CH), f32),
                        pltpu.VMEM((SUBLANES - 1, CONV_ROWS + HALO - SUBLANES, 128), f32)],
        compiler_params=_params(2),
        name="mixers",
    )(z, dww, dwb, clg, clb, pw_b, poolw_b, pscale, slg, slb, sgw, sgbt, mixg)


def _outproj_router_kernel(x_ref, y_ref, wout_ref, g1_ref, sc_ref, sh_ref, wrt_ref, rb_ref,
                           xo_ref, hp_ref, gate_ref, lpos_ref, tcnt_ref):
    tm = x_ref.shape[1]
    x = x_ref[0] + g1_ref[0] * jnp.dot(y_ref[0], wout_ref[...], preferred_element_type=f32)
    xo_ref[0] = x
    h = _rms(x) * (1.0 + sc_ref[0]) + sh_ref[0]
    h_hi = h.astype(bf16)
    h_hi32 = h_hi.astype(f32)
    h_lo = (h - h_hi32).astype(bf16)
    for s in range(PACK_S):
        hp_ref[_slab(tm, s), :] = _pack_pair(h_hi32[:, 256 * s:256 * s + 128],
                                             h_hi32[:, 256 * s + 128:256 * s + 256])

    wr = wrt_ref[...]
    w_hi = wr.astype(bf16)
    w_lo = (wr - w_hi.astype(f32)).astype(bf16)
    nt = (((1,), (1,)), ((), ()))
    lg = (lax.dot_general(w_hi, h_hi, nt, preferred_element_type=f32)
          + lax.dot_general(w_hi, h_lo, nt, preferred_element_type=f32)
          + lax.dot_general(w_lo, h_hi, nt, preferred_element_type=f32)
          + rb_ref[...])

    e_iota = lax.broadcasted_iota(i32, (N_EXP, tm), 0)
    vals, hots = [], []
    for _ in range(TOP_K):
        m = jnp.max(lg, axis=0, keepdims=True)
        idx = jnp.min(jnp.where(lg == m, e_iota, N_EXP), axis=0, keepdims=True)
        hot = e_iota == idx
        lg = jnp.where(hot, -jnp.inf, lg)
        vals.append(m)
        hots.append(hot)
    exps = [jnp.exp(v - vals[0]) for v in vals]
    denom = exps[0] + exps[1] + exps[2] + exps[3]
    gate_ref[...] = jnp.concatenate([e / denom for e in exps], axis=0)

    sel = jnp.zeros((N_EXP, tm), f32)
    for hot in hots:
        sel = sel + jnp.where(hot, 1.0, 0.0)
    sel_b = sel.astype(bf16)
    upper = jnp.where(lax.broadcasted_iota(i32, (tm, tm), 0) <= lax.broadcasted_iota(i32, (tm, tm), 1),
                      1.0, 0.0).astype(bf16)
    cum = jnp.dot(sel_b, upper, preferred_element_type=f32)
    below = jnp.where(lax.broadcasted_iota(i32, (N_EXP, N_EXP), 1)
                      < lax.broadcasted_iota(i32, (N_EXP, N_EXP), 0), 1.0, 0.0).astype(bf16)
    lower = jnp.sum(jnp.dot(below, sel_b, preferred_element_type=f32), axis=1, keepdims=True)
    place = lower + cum - sel
    lpos = [jnp.sum(jnp.where(hot, place, 0.0), axis=0, keepdims=True) for hot in hots]
    lpos_ref[...] = jnp.concatenate(lpos, axis=0).astype(i32) * PACK_S
    tcnt_ref[...] = jnp.broadcast_to(cum[:, tm - 1:tm], (N_EXP, 128))


def _outproj_router_call(l, x, y, w_out_b, g1, sc2, sh2, wrt, rb, tm):
    B, S, _ = x.shape
    T = B * S
    nj = S // tm
    tok = lambda b, j: (0, b * nj + j)
    return pl.pallas_call(
        _outproj_router_kernel,
        grid=(B, nj),
        in_specs=[pl.BlockSpec((1, tm, D), lambda b, j: (b, j, 0)),
                  pl.BlockSpec((1, tm, D), lambda b, j: (b, j, 0)),
                  _layer_resident(l, (D, D)),
                  pl.BlockSpec((1, 1, D), lambda b, j: (b, 0, 0)),
                  pl.BlockSpec((1, 1, D), lambda b, j: (b, 0, 0)),
                  pl.BlockSpec((1, 1, D), lambda b, j: (b, 0, 0)),
                  _layer_resident(l, (N_EXP, D)), _layer_resident(l, (N_EXP, 1))],
        out_specs=[pl.BlockSpec((1, tm, D), lambda b, j: (b, j, 0)),
                   pl.BlockSpec((tm * PACK_S, 128), lambda b, j: (b * nj + j, 0)),
                   pl.BlockSpec((TOP_K, tm), tok),
                   pl.BlockSpec((TOP_K, tm), tok),
                   pl.BlockSpec((None, N_EXP, 128), lambda b, j: (b * nj + j, 0, 0))],
        out_shape=[jax.ShapeDtypeStruct((B, S, D), f32),
                   jax.ShapeDtypeStruct((T * PACK_S, 128), u32),
                   jax.ShapeDtypeStruct((TOP_K, T), f32),
                   jax.ShapeDtypeStruct((TOP_K, T), i32),
                   jax.ShapeDtypeStruct((T // tm, N_EXP, 128), f32)],
        compiler_params=_params(2),
        name="out_proj_router",
    )(x, y, w_out_b, g1, sc2, sh2, wrt, rb)


def _permute_kernel(lpos_ref, h_ref, xs_ref):
    tm = h_ref.shape[0] // PACK_S
    unroll = 4

    def body(i, carry):
        for j in range(unroll):
            t = i * unroll + j
            v = h_ref[pl.ds(pl.multiple_of(t * PACK_S, PACK_S), PACK_S), :]
            for k in range(TOP_K):
                dst = pl.multiple_of(lpos_ref[0, TOP_K * t + k], PACK_S)
                xs_ref[pl.ds(dst, PACK_S), :] = v
        return carry

    lax.fori_loop(0, tm // unroll, body, 0)


def _permute_call(lpos_tiles, hp, tm):
    T = hp.shape[0] // PACK_S
    return pl.pallas_call(
        _permute_kernel,
        grid=(T // tm,),
        in_specs=[pl.BlockSpec((None, 1, TOP_K * tm), lambda i: (i, 0, 0), memory_space=pltpu.SMEM),
                  pl.BlockSpec((tm * PACK_S, 128), lambda i: (i, 0))],
        out_specs=pl.BlockSpec((TOP_K * tm * PACK_S, 128), lambda i: (i, 0)),
        out_shape=jax.ShapeDtypeStruct((TOP_K * T * PACK_S, 128), u32),
        compiler_params=_params(1),
        name="tile_permute",
    )(lpos_tiles, hp)


def _expert_kernel(be_ref, nu_ref, u0_ref, nv_ref, ilo_ref, ihi_ref, pref_ref, loff_ref,
                   xs_ref, w1_ref, b1_ref, w2_ref, b2_ref, ys_ref, xin, xbuf, sems, *, tile_rows):
    blk = pl.program_id(0)
    tm = xbuf.shape[0]
    n_used = nu_ref[0]

    def fetch(b, slot):
        e = be_ref[b]
        u0 = u0_ref[b]

        def seg(i, carry):
            s0 = pref_ref[i * N_EXP + e]
            s1 = pref_ref[(i + 1) * N_EXP + e]
            a = jnp.maximum(s0, u0)
            n = jnp.minimum(s1, u0 + tm) - a

            @pl.when(n > 0)
            def _():
                src = i * tile_rows + loff_ref[i * N_EXP + e] + (a - s0)
                pltpu.make_async_copy(xs_ref.at[_rows(src, n)], xin.at[slot, _rows(a - u0, n)],
                                      sems.at[slot]).start()
            return carry

        lax.fori_loop(ilo_ref[b], ihi_ref[b], seg, 0)

    @pl.when((blk == 0) & (n_used > 0))
    def _():
        fetch(0, 0)

    for slot in range(2):
        @pl.when((blk + 1 < n_used) & ((blk + 1) % 2 == slot))
        def _():
            fetch(blk + 1, slot)

    for slot in range(2):
        @pl.when((blk < n_used) & (blk % 2 == slot))
        def _():
            n_valid = nv_ref[blk]
            pltpu.make_async_copy(xs_ref.at[_rows(0, n_valid)], xin.at[slot, _rows(0, n_valid)],
                                  sems.at[slot]).wait()

            def clear(r, carry):
                xin[slot, _rows(r, 1), :] = jnp.zeros((PACK_S, 128), u32)
                return carry

            lax.fori_loop(n_valid, tm, clear, 0)
            for s in range(PACK_S):
                lo, hi = _unpack_pair(xin[slot, _slab(tm, s), :])
                xbuf[:, 256 * s:256 * s + 128] = lo.astype(bf16)
                xbuf[:, 256 * s + 128:256 * s + 256] = hi.astype(bf16)

    @pl.when(blk < n_used)
    def _():
        z = jnp.dot(xbuf[...], w1_ref[...], preferred_element_type=f32) + b1_ref[...]
        glu = jnp.minimum(z[:, 0:F_EXP], SWIGLU_LIMIT)
        lin = jnp.clip(z[:, F_EXP:], -SWIGLU_LIMIT, SWIGLU_LIMIT)
        a = glu * _sigmoid(SWIGLU_ALPHA * glu) * (lin + 1.0)
        o = jnp.dot(a.astype(bf16), w2_ref[...], preferred_element_type=f32) + b2_ref[...]
        for s in range(PACK_S):
            ys_ref[_slab(tm, s), :] = _pack_pair(o[:, 256 * s:256 * s + 128],
                                                 o[:, 256 * s + 128:256 * s + 256])

    @pl.when(blk >= n_used)
    def _():
        ys_ref[...] = jnp.zeros_like(ys_ref)


def _expert_call(l, tables, xs, w1_b, b1, w2_b, b2, n_blocks, tm, tile_rows):
    n_pref = len(tables)
    per_expert = lambda shape: pl.BlockSpec((None, None) + shape,
                                            lambda i, be, *_: (l, be[i]) + (0,) * len(shape))
    return pl.pallas_call(
        functools.partial(_expert_kernel, tile_rows=tile_rows),
        grid_spec=pltpu.PrefetchScalarGridSpec(
            num_scalar_prefetch=n_pref,
            grid=(n_blocks,),
            in_specs=[pl.BlockSpec(memory_space=pl.ANY),
                      per_expert((D, 2 * F_EXP)), per_expert((1, 2 * F_EXP)),
                      per_expert((F_EXP, D)), per_expert((1, D))],
            out_specs=pl.BlockSpec((tm * PACK_S, 128), lambda i, *_: (i, 0)),
            scratch_shapes=[pltpu.VMEM((2, tm * PACK_S, 128), u32),
                            pltpu.VMEM((tm, D), bf16),
                            pltpu.SemaphoreType.DMA((2,))]),
        out_shape=jax.ShapeDtypeStruct((n_blocks * tm * PACK_S, 128), u32),
        compiler_params=_params(1),
        name="expert_ffn",
    )(*tables, xs, w1_b, b1, w2_b, b2)


def _combine_kernel(cnt_ref, src_ref, loff_ref, lpos_ref, gate_ref, x_ref, g2_ref, fg_ref, ys_ref,
                    o_ref, gbuf, flo, fhi, sems, *, final):
    tm = x_ref.shape[1]
    step = pl.program_id(0) * pl.num_programs(1) + pl.program_id(1)
    n_steps = pl.num_programs(0) * pl.num_programs(1)

    def fetch(tile, slot):
        for e in range(N_EXP):
            n = cnt_ref[tile * N_EXP + e]

            @pl.when(n > 0)
            def _():
                pltpu.make_async_copy(ys_ref.at[_rows(src_ref[tile * N_EXP + e], n)],
                                      gbuf.at[slot, _rows(loff_ref[tile * N_EXP + e], n)],
                                      sems.at[slot]).start()

    @pl.when(step == 0)
    def _():
        fetch(0, 0)

    for slot in range(2):
        @pl.when((step + 1 < n_steps) & ((step + 1) % 2 == slot))
        def _():
            fetch(step + 1, slot)

    unroll = 2
    for slot in range(2):
        @pl.when(step % 2 == slot)
        def _():
            pltpu.make_async_copy(ys_ref.at[_rows(0, TOP_K * tm)], gbuf.at[slot], sems.at[slot]).wait()

            def body(i, carry):
                for j in range(unroll):
                    t = i * unroll + j
                    acc_lo = jnp.zeros((PACK_S, 128), f32)
                    acc_hi = jnp.zeros((PACK_S, 128), f32)
                    for k in range(TOP_K):
                        src = pl.multiple_of(lpos_ref[0, TOP_K * t + k], PACK_S)
                        lo, hi = _unpack_pair(gbuf[slot, pl.ds(src, PACK_S), :])
                        g = gate_ref[0, TOP_K * t + k]
                        acc_lo = acc_lo + g * lo
                        acc_hi = acc_hi + g * hi
                    flo[_rows(t, 1), :] = acc_lo
                    fhi[_rows(t, 1), :] = acc_hi
                return carry

            lax.fori_loop(0, tm // unroll, body, 0)

    for s in range(PACK_S):
        for half, fref in enumerate((flo, fhi)):
            cols = slice(256 * s + 128 * half, 256 * s + 128 * half + 128)
            o_ref[0, :, cols] = x_ref[0, :, cols] + g2_ref[0, :, cols] * fref[_slab(tm, s), :]
    if final:
        o_ref[0] = _rms(o_ref[0]) * fg_ref[...]


def _combine_call(tables, lpos_tiles, gate_tiles, x, g2, final_g, ys, tm, final):
    B, S, _ = x.shape
    nj = S // tm
    smem_tile = lambda: pl.BlockSpec((None, 1, TOP_K * tm), lambda b, j, *_: (b * nj + j, 0, 0),
                                     memory_space=pltpu.SMEM)
    return pl.pallas_call(
        functools.partial(_combine_kernel, final=final),
        grid_spec=pltpu.PrefetchScalarGridSpec(
            num_scalar_prefetch=len(tables),
            grid=(B, nj),
            in_specs=[smem_tile(), smem_tile(),
                      pl.BlockSpec((1, tm, D), lambda b, j, *_: (b, j, 0)),
                      pl.BlockSpec((1, 1, D), lambda b, j, *_: (b, 0, 0)),
                      pl.BlockSpec((1, D), lambda b, j, *_: (0, 0)),
                      pl.BlockSpec(memory_space=pl.ANY)],
            out_specs=pl.BlockSpec((1, tm, D), lambda b, j, *_: (b, j, 0)),
            scratch_shapes=[pltpu.VMEM((2, TOP_K * tm * PACK_S, 128), u32),
                            pltpu.VMEM((tm * PACK_S, 128), f32),
                            pltpu.VMEM((tm * PACK_S, 128), f32),
                            pltpu.SemaphoreType.DMA((2,))]),
        out_shape=jax.ShapeDtypeStruct((B, S, D), f32),
        compiler_params=_params(2),
        name="combine",
    )(*tables, lpos_tiles, gate_tiles, x, g2, final_g, ys)


def _tile(n, want):
    t = min(n, want)
    assert n % t == 0 and t % CHUNK == 0, (n, t)
    return t


def _routing_tables(tcnt, tm_tok, tm_exp, n_blocks):
    n_tiles = tcnt.shape[0]
    cnt = tcnt[:, :, 0].astype(i32)
    pref = jnp.concatenate([jnp.zeros((1, N_EXP), i32), jnp.cumsum(cnt, axis=0)])
    loff = jnp.cumsum(cnt, axis=1) - cnt
    counts = pref[-1]
    padded = (counts + tm_exp - 1) // tm_exp * tm_exp
    pad_ends = jnp.cumsum(padded)
    pad_starts = pad_ends - padded
    n_used = (pad_ends[-1:] // tm_exp).astype(i32)
    row0 = jnp.arange(n_blocks, dtype=i32) * tm_exp
    block_e = jnp.minimum(jnp.sum(row0[:, None] >= pad_ends[None, :], axis=1), N_EXP - 1).astype(i32)
    u0 = row0 - pad_starts[block_e]
    n_valid = jnp.where(row0 < pad_ends[-1], jnp.clip(counts[block_e] - u0, 0, tm_exp), 0).astype(i32)
    pref_e = pref[:, block_e].T
    tile_lo = jnp.sum(pref_e[:, 1:] <= u0[:, None], axis=1).astype(i32)
    tile_hi = jnp.sum(pref_e[:, :-1] < (u0 + tm_exp)[:, None], axis=1).astype(i32)
    expert_tables = (block_e, n_used, u0.astype(i32), n_valid, tile_lo, tile_hi,
                     pref.reshape(-1), loff.reshape(-1).astype(i32))
    src = pad_starts[None, :] + pref[:-1]
    combine_tables = (cnt.reshape(-1), src.reshape(-1).astype(i32), loff.reshape(-1).astype(i32))
    return expert_tables, combine_tables


def kernel(x, c, ada_w, ada_b, w_in, conv_dw_w, conv_dw_b, conv_ln_g, conv_ln_b, conv_pw_w,
           pool_w, pool_scale, sg_ln_g, sg_ln_b, sg_w, sg_b, mix_norm_g, w_out,
           router_w, router_b, moe_w1, moe_b1, moe_w2, moe_b2, final_g):
    B, S, d_model = x.shape
    L = ada_w.shape[0]
    assert d_model == D and w_in.shape[-1] == IN_COLS and moe_w1.shape[1] == N_EXP
    T = B * S
    tm_in = _tile(S, 512)
    tm_mix = _tile(S, 256)
    tm_tok = _tile(S, 512)
    tm_exp = _tile(S, 512)
    n_blocks = -(-(T * TOP_K) // tm_exp) + N_EXP

    w_in_b, w_out_b, pw_b, poolw_b = (w.astype(bf16) for w in (w_in, w_out, conv_pw_w, pool_w))
    w1_b, w2_b = moe_w1.astype(bf16), moe_w2.astype(bf16)
    rows3 = lambda a: a.reshape(L, 1, -1)
    dwb, clg, clb, pscale, slg, slb, mixg = (
        rows3(a) for a in (conv_dw_b, conv_ln_g, conv_ln_b, pool_scale, sg_ln_g, sg_ln_b, mix_norm_g))
    sgbt = jnp.swapaxes(sg_b, 1, 2)
    wrt = jnp.swapaxes(router_w, 1, 2)
    rb = router_b.reshape(L, N_EXP, 1)
    b1 = moe_b1.reshape(L, N_EXP, 1, 2 * F_EXP)
    b2 = moe_b2.reshape(L, N_EXP, 1, D)
    fg = final_g.reshape(1, D)
    per_tile = lambda a: a.T.reshape(T // tm_tok, 1, TOP_K * tm_tok)

    mod = _ada_call(c, ada_w, ada_b)
    for l in range(L):
        sh1, sc1, g1, sh2, sc2, g2 = [m.reshape(B, 1, D) for m in jnp.split(mod[l], 6, axis=-1)]

        z = _inproj_call(l, x, sc1, sh1, w_in_b, tm_in)
        y = _mixer_call(l, z, conv_dw_w, dwb, clg, clb, pw_b, poolw_b, pscale, slg, slb, sg_w,
                        sgbt, mixg, tm_mix)
        x, hp, gates, lpos, tcnt = _outproj_router_call(l, x, y, w_out_b, g1, sc2, sh2, wrt, rb, tm_tok)

        expert_tables, combine_tables = _routing_tables(tcnt, tm_tok, tm_exp, n_blocks)
        lpos_tiles = per_tile(lpos)
        xs = _permute_call(lpos_tiles, hp, tm_tok)
        ys = _expert_call(l, expert_tables, xs, w1_b, b1, w2_b, b2, n_blocks, tm_exp,
                          TOP_K * tm_tok)
        x = _combine_call(combine_tables, lpos_tiles, per_tile(gates), x, g2, fg, ys, tm_tok,
                          final=(l == L - 1))
    return x
```

```python
import functools

import jax
import jax.numpy as jnp
from jax import lax
from jax.experimental import pallas as pl
from jax.experimental.pallas import tpu as pltpu

f32 = jnp.float32
bf16 = jnp.bfloat16
u32 = jnp.uint32
i32 = jnp.int32

D = 2048
HEAD = 128
CONV_CH = 768
POOL_CH = 512
GATE_CH = 768
IN_COLS = 2 * CONV_CH + POOL_CH + 2 * GATE_CH
POOL_OFF = 2 * CONV_CH
GATE_OFF = 2 * CONV_CH + POOL_CH
POOL_WINDOWS = (2, 4, 8, 16)
CONV_W = 31
CHUNK = 128
N_EXP = 32
TOP_K = 4
F_EXP = 768
SWIGLU_ALPHA = 1.702
SWIGLU_LIMIT = 7.0
EPS = 1e-5

SUBLANES = 8
HALO = 32
CONV_ROWS = 64
PACK_W = D // 2
PACK_S = PACK_W // 128
assert PACK_S == SUBLANES
VMEM_LIMIT = 56 * 1024 * 1024


def _sigmoid(v):
    return 1.0 / (1.0 + jnp.exp(-v))


def _rms(v):
    return v * lax.rsqrt(jnp.mean(v * v, axis=-1, keepdims=True) + EPS)


def _layernorm(v, g, b):
    mu = jnp.mean(v, axis=-1, keepdims=True)
    d = v - mu
    var = jnp.mean(d * d, axis=-1, keepdims=True)
    return d * lax.rsqrt(var + EPS) * g + b


def _pack_pair(lo, hi):
    lo_bits = lax.bitcast_convert_type(lo.astype(bf16).astype(f32), u32)
    hi_bits = lax.bitcast_convert_type(hi.astype(bf16).astype(f32), u32)
    return (lo_bits >> 16) | (hi_bits & jnp.uint32(0xFFFF0000))


def _unpack_pair(w):
    lo = lax.bitcast_convert_type(w << 16, f32)
    hi = lax.bitcast_convert_type(w & jnp.uint32(0xFFFF0000), f32)
    return lo, hi


def _slab(n_tokens, s):
    return pl.ds(s, n_tokens, stride=PACK_S)


def _rows(first_row, n_rows):
    start = first_row * PACK_S
    if not isinstance(start, int):
        start = pl.multiple_of(start, PACK_S)
    return pl.ds(start, n_rows * PACK_S)


def _params(n_axes):
    return pltpu.CompilerParams(dimension_semantics=("arbitrary",) * n_axes,
                                vmem_limit_bytes=VMEM_LIMIT)


def _layer_resident(l, shape):
    nd = len(shape)
    return pl.BlockSpec((None,) + tuple(shape), lambda *_: (l,) + (0,) * nd,
                        pipeline_mode=pl.Buffered(1))


def _ada_kernel(c_ref, w_ref, b_ref, o_ref):
    c = c_ref[...]
    ca = (c * _sigmoid(c)).astype(bf16)
    o_ref[0] = jnp.dot(ca, w_ref[0].astype(bf16), preferred_element_type=f32) + b_ref[0]


def _ada_call(c, ada_w, ada_b):
    L, _, n6 = ada_w.shape
    B = c.shape[0]
    tn = 1024
    return pl.pallas_call(
        _ada_kernel,
        grid=(L, n6 // tn),
        in_specs=[pl.BlockSpec((B, D), lambda l, n: (0, 0)),
                  pl.BlockSpec((1, D, tn), lambda l, n: (l, 0, n)),
                  pl.BlockSpec((1, 1, tn), lambda l, n: (l, 0, n))],
        out_specs=pl.BlockSpec((1, B, tn), lambda l, n: (l, 0, n)),
        out_shape=jax.ShapeDtypeStruct((L, B, n6), f32),
        compiler_params=_params(2),
        name="ada_mod",
    )(c, ada_w, ada_b.reshape(L, 1, n6))


IN_PROJ_COLS = 512


def _mixer_kernel(x_ref, sc_ref, sh_ref, win_ref, dww_ref, dwb_ref, clg_ref, clb_ref, pw_ref,
                  poolw_ref, pscale_ref, slg_ref, slb_ref, sgw_ref, sgbt_ref, mixg_ref, y_ref,
                  znew, zbuf, abuf, pbuf, cbuf, shbuf, *, tiles_per_seq):
    step = pl.program_id(0)
    tm = y_ref.shape[1]
    j = (step + tiles_per_seq - 1) % tiles_per_seq

    @pl.when(step == 0)
    def _():
        zbuf[...] = jnp.zeros((tm, IN_COLS), bf16)

    @pl.when(j == 0)
    def _():
        abuf[0:HALO, :] = jnp.zeros((HALO, CONV_CH), f32)
        pbuf[0:HALO, :] = jnp.zeros((HALO, POOL_CH), f32)

    h = (_rms(x_ref[0]) * (1.0 + sc_ref[0]) + sh_ref[0]).astype(bf16)
    proj_chunks = list(range(0, IN_COLS, IN_PROJ_COLS))

    def project(n_chunks):
        for _ in range(min(n_chunks, len(proj_chunks))):
            n0 = proj_chunks.pop(0)
            znew[:, n0:n0 + IN_PROJ_COLS] = jnp.dot(
                h, win_ref[:, n0:n0 + IN_PROJ_COLS], preferred_element_type=f32).astype(bf16)

    za = zbuf[:, 0:CONV_CH].astype(f32)
    zg = zbuf[:, CONV_CH:2 * CONV_CH].astype(f32)
    abuf[HALO:HALO + tm, :] = za * _sigmoid(zg)
    first = HALO - (CONV_W - 1)
    span = CONV_ROWS + HALO
    for r0 in range(0, tm, CONV_ROWS):
        if (r0 // CONV_ROWS) % 2 == 0:
            project(1)
        for c0 in range(0, CONV_CH, 128):
            win = abuf[r0:r0 + span, c0:c0 + 128]
            for res in range(1, SUBLANES):
                shbuf[res - 1] = win[res:res + span - SUBLANES, :]
            acc = jnp.broadcast_to(dwb_ref[:, c0:c0 + 128], (CONV_ROWS, 128))
            for t in range(CONV_W):
                q, res = divmod(first + t, SUBLANES)
                rows = slice(SUBLANES * q, SUBLANES * q + CONV_ROWS)
                src = win[rows, :] if res == 0 else shbuf[res - 1, rows, :]
                acc = acc + dww_ref[t:t + 1, c0:c0 + 128] * src
            cbuf[r0:r0 + CONV_ROWS, c0:c0 + 128] = acc
    abuf[0:HALO, :] = abuf[tm:tm + HALO, :]
    project(1)
    an = _layernorm(cbuf[...], clg_ref[...], clb_ref[...])
    an = an * _sigmoid(an)
    ya = _rms(jnp.dot(an.astype(bf16), pw_ref[...], preferred_element_type=f32))
    y_ref[0, :, 0:CONV_CH] = (ya * mixg_ref[:, 0:CONV_CH]).astype(bf16)

    p = zbuf[:, POOL_OFF:POOL_OFF + POOL_CH].astype(f32)
    pbuf[HALO:HALO + tm, :] = p
    pos = (j * tm + 1 + lax.broadcasted_iota(i32, (tm, 1), 0)).astype(f32)
    ybs = []
    for g, w in enumerate(POOL_WINDOWS):
        c0 = g * HEAD
        win = pbuf[HALO:HALO + tm, c0:c0 + HEAD]
        for i in range(1, w):
            win = win + pbuf[HALO - i:HALO - i + tm, c0:c0 + HEAD]
        diff = win / jnp.minimum(pos, float(w)) - p[:, c0:c0 + HEAD]
        yg = jnp.dot(diff.astype(bf16), poolw_ref[g], preferred_element_type=f32)
        ybs.append(yg * pscale_ref[:, c0:c0 + HEAD])
    pbuf[0:HALO, :] = pbuf[tm:tm + HALO, :]
    project(1)
    yb = _rms(jnp.concatenate(ybs, axis=-1))
    y_ref[0, :, CONV_CH:CONV_CH + POOL_CH] = (yb * mixg_ref[:, CONV_CH:CONV_CH + POOL_CH]).astype(bf16)

    zc = zbuf[:, GATE_OFF:GATE_OFF + 2 * GATE_CH].astype(f32)
    gl = 0.5 * zc * (1.0 + jnp.tanh(0.7978845608028654 * (zc + 0.044715 * zc * zc * zc)))
    u = gl[:, 0:GATE_CH]
    vn = _layernorm(gl[:, GATE_CH:], slg_ref[...], slb_ref[...]).astype(bf16)
    causal = (lax.broadcasted_iota(i32, (CHUNK, CHUNK), 1)
              <= lax.broadcasted_iota(i32, (CHUNK, CHUNK), 0))
    for hh in range(GATE_CH // HEAD):
        c0 = hh * HEAD
        wm = jnp.where(causal, sgw_ref[hh], 0.0).astype(bf16)
        bias = sgbt_ref[:, hh:hh + 1]
        for n in range(tm // CHUNK):
            r0 = n * CHUNK
            s = jnp.dot(wm, vn[r0:r0 + CHUNK, c0:c0 + HEAD], preferred_element_type=f32) + bias
            cbuf[r0:r0 + CHUNK, c0:c0 + HEAD] = u[r0:r0 + CHUNK, c0:c0 + HEAD] * s
    project(len(proj_chunks))
    yc = _rms(cbuf[...])
    y_ref[0, :, CONV_CH + POOL_CH:] = (yc * mixg_ref[:, CONV_CH + POOL_CH:]).astype(bf16)

    zbuf[...] = znew[...]


def _mixer_call(l, x, sc, sh, w_in_b, dww, dwb, clg, clb, pw_b, poolw_b, pscale, slg, slb, sgw, sgbt,
                mixg, tm):
    B, S, _ = x.shape
    nj = S // tm
    n_tiles = B * nj
    n_heads = GATE_CH // HEAD
    lr = functools.partial(_layer_resident, l)
    proj = lambda s: jnp.minimum(s, n_tiles - 1)
    mixed = lambda s: jnp.maximum(s - 1, 0)
    return pl.pallas_call(
        functools.partial(_mixer_kernel, tiles_per_seq=nj),
        grid=(n_tiles + 1,),
        in_specs=[pl.BlockSpec((1, tm, D), lambda s: (proj(s) // nj, proj(s) % nj, 0)),
                  pl.BlockSpec((1, 1, D), lambda s: (proj(s) // nj, 0, 0)),
                  pl.BlockSpec((1, 1, D), lambda s: (proj(s) // nj, 0, 0)),
                  lr((D, IN_COLS)),
                  lr((CONV_W, CONV_CH)), lr((1, CONV_CH)), lr((1, CONV_CH)), lr((1, CONV_CH)),
                  lr((CONV_CH, CONV_CH)),
                  lr((len(POOL_WINDOWS), HEAD, HEAD)), lr((1, POOL_CH)),
                  lr((1, GATE_CH)), lr((1, GATE_CH)),
                  lr((n_heads, CHUNK, CHUNK)), lr((CHUNK, n_heads)),
                  lr((1, D))],
        out_specs=pl.BlockSpec((1, tm, D), lambda s: (mixed(s) // nj, mixed(s) % nj, 0)),
        out_shape=jax.ShapeDtypeStruct((B, S, D), bf16),
        scratch_shapes=[pltpu.VMEM((tm, IN_COLS), bf16),
                        pltpu.VMEM((tm, IN_COLS), bf16),
                        pltpu.VMEM((HALO + tm, CONV_CH), f32),
                        pltpu.VMEM((HALO + tm, POOL_CH), f32),
                        pltpu.VMEM((tm, CONV_CH), f32),
                        pltpu.VMEM((SUBLANES - 1, CONV_ROWS + HALO - SUBLANES, 128), f32)],
        compiler_params=_params(1),
        name="in_proj_mixers",
    )(x, sc, sh, w_in_b, dww, dwb, clg, clb, pw_b, poolw_b, pscale, slg, slb, sgw, sgbt, mixg)


def _outproj_router_kernel(x_ref, y_ref, wout_ref, g1_ref, sc_ref, sh_ref, wrt_ref, rb_ref,
                           xo_ref, hp_ref, gate_ref, lpos_ref, tcnt_ref):
    tm = x_ref.shape[1]
    x = x_ref[0] + g1_ref[0] * jnp.dot(y_ref[0], wout_ref[...], preferred_element_type=f32)
    xo_ref[0] = x
    h = _rms(x) * (1.0 + sc_ref[0]) + sh_ref[0]
    h_hi = h.astype(bf16)
    h_hi32 = h_hi.astype(f32)
    h_lo = (h - h_hi32).astype(bf16)
    for s in range(PACK_S):
        hp_ref[_slab(tm, s), :] = _pack_pair(h_hi32[:, 256 * s:256 * s + 128],
                                             h_hi32[:, 256 * s + 128:256 * s + 256])

    wr = wrt_ref[...]
    w_hi = wr.astype(bf16)
    w_lo = (wr - w_hi.astype(f32)).astype(bf16)
    nt = (((1,), (1,)), ((), ()))
    lg = (lax.dot_general(w_hi, h_hi, nt, preferred_element_type=f32)
          + lax.dot_general(w_hi, h_lo, nt, preferred_element_type=f32)
          + lax.dot_general(w_lo, h_hi, nt, preferred_element_type=f32)
          + rb_ref[...])

    e_iota = lax.broadcasted_iota(i32, (N_EXP, tm), 0)
    vals, hots = [], []
    for _ in range(TOP_K):
        m = jnp.max(lg, axis=0, keepdims=True)
        idx = jnp.min(jnp.where(lg == m, e_iota, N_EXP), axis=0, keepdims=True)
        hot = e_iota == idx
        lg = jnp.where(hot, -jnp.inf, lg)
        vals.append(m)
        hots.append(hot)
    exps = [jnp.exp(v - vals[0]) for v in vals]
    denom = exps[0] + exps[1] + exps[2] + exps[3]
    gate_ref[...] = jnp.concatenate([e / denom for e in exps], axis=0)

    sel = jnp.zeros((N_EXP, tm), f32)
    for hot in hots:
        sel = sel + jnp.where(hot, 1.0, 0.0)
    sel_b = sel.astype(bf16)
    upper = jnp.where(lax.broadcasted_iota(i32, (tm, tm), 0) <= lax.broadcasted_iota(i32, (tm, tm), 1),
                      1.0, 0.0).astype(bf16)
    cum = jnp.dot(sel_b, upper, preferred_element_type=f32)
    below = jnp.where(lax.broadcasted_iota(i32, (N_EXP, N_EXP), 1)
                      < lax.broadcasted_iota(i32, (N_EXP, N_EXP), 0), 1.0, 0.0).astype(bf16)
    lower = jnp.sum(jnp.dot(below, sel_b, preferred_element_type=f32), axis=1, keepdims=True)
    place = lower + cum - sel
    lpos = [jnp.sum(jnp.where(hot, place, 0.0), axis=0, keepdims=True) for hot in hots]
    lpos_ref[...] = jnp.concatenate(lpos, axis=0).astype(i32) * PACK_S
    tcnt_ref[...] = jnp.broadcast_to(cum[:, tm - 1:tm], (N_EXP, 128))


def _outproj_router_call(l, x, y, w_out_b, g1, sc2, sh2, wrt, rb, tm):
    B, S, _ = x.shape
    T = B * S
    nj = S // tm
    tok = lambda b, j: (0, b * nj + j)
    return pl.pallas_call(
        _outproj_router_kernel,
        grid=(B, nj),
        in_specs=[pl.BlockSpec((1, tm, D), lambda b, j: (b, j, 0)),
                  pl.BlockSpec((1, tm, D), lambda b, j: (b, j, 0)),
                  _layer_resident(l, (D, D)),
                  pl.BlockSpec((1, 1, D), lambda b, j: (b, 0, 0)),
                  pl.BlockSpec((1, 1, D), lambda b, j: (b, 0, 0)),
                  pl.BlockSpec((1, 1, D), lambda b, j: (b, 0, 0)),
                  _layer_resident(l, (N_EXP, D)), _layer_resident(l, (N_EXP, 1))],
        out_specs=[pl.BlockSpec((1, tm, D), lambda b, j: (b, j, 0)),
                   pl.BlockSpec((tm * PACK_S, 128), lambda b, j: (b * nj + j, 0)),
                   pl.BlockSpec((TOP_K, tm), tok),
                   pl.BlockSpec((TOP_K, tm), tok),
                   pl.BlockSpec((None, N_EXP, 128), lambda b, j: (b * nj + j, 0, 0))],
        out_shape=[jax.ShapeDtypeStruct((B, S, D), f32),
                   jax.ShapeDtypeStruct((T * PACK_S, 128), u32),
                   jax.ShapeDtypeStruct((TOP_K, T), f32),
                   jax.ShapeDtypeStruct((TOP_K, T), i32),
                   jax.ShapeDtypeStruct((T // tm, N_EXP, 128), f32)],
        compiler_params=_params(2),
        name="out_proj_router",
    )(x, y, w_out_b, g1, sc2, sh2, wrt, rb)


def _permute_kernel(lpos_ref, h_ref, xs_ref):
    tm = h_ref.shape[0] // PACK_S
    unroll = 4

    def body(i, carry):
        for j in range(unroll):
            t = i * unroll + j
            v = h_ref[pl.ds(pl.multiple_of(t * PACK_S, PACK_S), PACK_S), :]
            for k in range(TOP_K):
                dst = pl.multiple_of(lpos_ref[k, t], PACK_S)
                xs_ref[pl.ds(dst, PACK_S), :] = v
        return carry

    lax.fori_loop(0, tm // unroll, body, 0)


def _permute_call(lpos_tiles, hp, tm):
    T = hp.shape[0] // PACK_S
    return pl.pallas_call(
        _permute_kernel,
        grid=(T // tm,),
        in_specs=[pl.BlockSpec((TOP_K, tm), lambda i: (0, i), memory_space=pltpu.SMEM),
                  pl.BlockSpec((tm * PACK_S, 128), lambda i: (i, 0))],
        out_specs=pl.BlockSpec((TOP_K * tm * PACK_S, 128), lambda i: (i, 0)),
        out_shape=jax.ShapeDtypeStruct((TOP_K * T * PACK_S, 128), u32),
        compiler_params=_params(1),
        name="tile_permute",
    )(lpos_tiles, hp)


def _expert_kernel(be_ref, nu_ref, u0_ref, nv_ref, ilo_ref, ihi_ref, pref_ref, loff_ref,
                   xs_ref, w1_ref, b1_ref, w2_ref, b2_ref, ys_ref, xin, xbuf, sems, *, tile_rows):
    blk = pl.program_id(0)
    tm = xbuf.shape[0]
    n_used = nu_ref[0]

    def fetch(b, slot):
        e = be_ref[b]
        u0 = u0_ref[b]

        def seg(i, carry):
            s0 = pref_ref[i * N_EXP + e]
            s1 = pref_ref[(i + 1) * N_EXP + e]
            a = jnp.maximum(s0, u0)
            n = jnp.minimum(s1, u0 + tm) - a

            @pl.when(n > 0)
            def _():
                src = i * tile_rows + loff_ref[i * N_EXP + e] + (a - s0)
                pltpu.make_async_copy(xs_ref.at[_rows(src, n)], xin.at[slot, _rows(a - u0, n)],
                                      sems.at[slot]).start()
            return carry

        lax.fori_loop(ilo_ref[b], ihi_ref[b], seg, 0)

    @pl.when((blk == 0) & (n_used > 0))
    def _():
        fetch(0, 0)

    @pl.when(blk + 1 < n_used)
    def _():
        fetch(blk + 1, (blk + 1) % 2)

    @pl.when(blk < n_used)
    def _():
        slot = blk % 2
        n_valid = nv_ref[blk]
        pltpu.make_async_copy(xs_ref.at[_rows(0, n_valid)], xin.at[slot, _rows(0, n_valid)],
                              sems.at[slot]).wait()

        def clear(r, carry):
            xin[slot, _rows(r, 1), :] = jnp.zeros((PACK_S, 128), u32)
            return carry

        lax.fori_loop(n_valid, tm, clear, 0)
        for s in range(PACK_S):
            lo, hi = _unpack_pair(xin[slot, _slab(tm, s), :])
            xbuf[:, 256 * s:256 * s + 128] = lo.astype(bf16)
            xbuf[:, 256 * s + 128:256 * s + 256] = hi.astype(bf16)
        z = jnp.dot(xbuf[...], w1_ref[...], preferred_element_type=f32) + b1_ref[...]
        glu = jnp.minimum(z[:, 0:F_EXP], SWIGLU_LIMIT)
        lin = jnp.clip(z[:, F_EXP:], -SWIGLU_LIMIT, SWIGLU_LIMIT)
        a = glu * _sigmoid(SWIGLU_ALPHA * glu) * (lin + 1.0)
        o = jnp.dot(a.astype(bf16), w2_ref[...], preferred_element_type=f32) + b2_ref[...]
        for s in range(PACK_S):
            ys_ref[_slab(tm, s), :] = _pack_pair(o[:, 256 * s:256 * s + 128],
                                                 o[:, 256 * s + 128:256 * s + 256])

    @pl.when(blk >= n_used)
    def _():
        ys_ref[...] = jnp.zeros_like(ys_ref)


def _expert_call(l, tables, xs, w1_b, b1, w2_b, b2, n_blocks, tm, tile_rows):
    n_pref = len(tables)
    per_expert = lambda shape: pl.BlockSpec((None, None) + shape,
                                            lambda i, be, *_: (l, be[i]) + (0,) * len(shape))
    return pl.pallas_call(
        functools.partial(_expert_kernel, tile_rows=tile_rows),
        grid_spec=pltpu.PrefetchScalarGridSpec(
            num_scalar_prefetch=n_pref,
            grid=(n_blocks,),
            in_specs=[pl.BlockSpec(memory_space=pl.ANY),
                      per_expert((D, 2 * F_EXP)), per_expert((1, 2 * F_EXP)),
                      per_expert((F_EXP, D)), per_expert((1, D))],
            out_specs=pl.BlockSpec((tm * PACK_S, 128), lambda i, *_: (i, 0)),
            scratch_shapes=[pltpu.VMEM((2, tm * PACK_S, 128), u32),
                            pltpu.VMEM((tm, D), bf16),
                            pltpu.SemaphoreType.DMA((2,))]),
        out_shape=jax.ShapeDtypeStruct((n_blocks * tm * PACK_S, 128), u32),
        compiler_params=_params(1),
        name="expert_ffn",
    )(*tables, xs, w1_b, b1, w2_b, b2)


def _combine_kernel(cnt_ref, src_ref, loff_ref, lpos_ref, gate_ref, x_ref, g2_ref, fg_ref, ys_ref,
                    o_ref, gbuf, flo, fhi, sems, *, final):
    tm = x_ref.shape[1]
    step = pl.program_id(0) * pl.num_programs(1) + pl.program_id(1)
    n_steps = pl.num_programs(0) * pl.num_programs(1)

    def fetch(tile, slot):
        for e in range(N_EXP):
            n = cnt_ref[tile * N_EXP + e]

            @pl.when(n > 0)
            def _():
                pltpu.make_async_copy(ys_ref.at[_rows(src_ref[tile * N_EXP + e], n)],
                                      gbuf.at[slot, _rows(loff_ref[tile * N_EXP + e], n)],
                                      sems.at[slot]).start()

    @pl.when(step == 0)
    def _():
        fetch(0, 0)

    for slot in range(2):
        @pl.when((step + 1 < n_steps) & ((step + 1) % 2 == slot))
        def _():
            fetch(step + 1, slot)

    unroll = 2
    for slot in range(2):
        @pl.when(step % 2 == slot)
        def _():
            pltpu.make_async_copy(ys_ref.at[_rows(0, TOP_K * tm)], gbuf.at[slot], sems.at[slot]).wait()

            def body(i, carry):
                for j in range(unroll):
                    t = i * unroll + j
                    acc_lo = jnp.zeros((PACK_S, 128), f32)
                    acc_hi = jnp.zeros((PACK_S, 128), f32)
                    for k in range(TOP_K):
                        src = pl.multiple_of(lpos_ref[k, t], PACK_S)
                        lo, hi = _unpack_pair(gbuf[slot, pl.ds(src, PACK_S), :])
                        g = gate_ref[k, t]
                        acc_lo = acc_lo + g * lo
                        acc_hi = acc_hi + g * hi
                    flo[_rows(t, 1), :] = acc_lo
                    fhi[_rows(t, 1), :] = acc_hi
                return carry

            lax.fori_loop(0, tm // unroll, body, 0)

    for s in range(PACK_S):
        for half, fref in enumerate((flo, fhi)):
            cols = slice(256 * s + 128 * half, 256 * s + 128 * half + 128)
            o_ref[0, :, cols] = x_ref[0, :, cols] + g2_ref[0, :, cols] * fref[_slab(tm, s), :]
    if final:
        o_ref[0] = _rms(o_ref[0]) * fg_ref[...]


def _combine_call(tables, lpos_tiles, gate_tiles, x, g2, final_g, ys, tm, final):
    B, S, _ = x.shape
    nj = S // tm
    smem_tile = lambda: pl.BlockSpec((TOP_K, tm), lambda b, j, *_: (0, b * nj + j),
                                     memory_space=pltpu.SMEM)
    return pl.pallas_call(
        functools.partial(_combine_kernel, final=final),
        grid_spec=pltpu.PrefetchScalarGridSpec(
            num_scalar_prefetch=len(tables),
            grid=(B, nj),
            in_specs=[smem_tile(), smem_tile(),
                      pl.BlockSpec((1, tm, D), lambda b, j, *_: (b, j, 0)),
                      pl.BlockSpec((1, 1, D), lambda b, j, *_: (b, 0, 0)),
                      pl.BlockSpec((1, D), lambda b, j, *_: (0, 0)),
                      pl.BlockSpec(memory_space=pl.ANY)],
            out_specs=pl.BlockSpec((1, tm, D), lambda b, j, *_: (b, j, 0)),
            scratch_shapes=[pltpu.VMEM((2, TOP_K * tm * PACK_S, 128), u32),
                            pltpu.VMEM((tm * PACK_S, 128), f32),
                            pltpu.VMEM((tm * PACK_S, 128), f32),
                            pltpu.SemaphoreType.DMA((2,))]),
        out_shape=jax.ShapeDtypeStruct((B, S, D), f32),
        compiler_params=_params(2),
        name="combine",
    )(*tables, lpos_tiles, gate_tiles, x, g2, final_g, ys)


def _tile(n, want):
    t = min(n, want)
    assert n % t == 0 and t % CHUNK == 0, (n, t)
    return t


def _routing_tables(tcnt, tm_tok, tm_exp, n_blocks):
    n_tiles = tcnt.shape[0]
    cnt = tcnt[:, :, 0].astype(i32)
    pref = jnp.concatenate([jnp.zeros((1, N_EXP), i32), jnp.cumsum(cnt, axis=0)])
    loff = jnp.cumsum(cnt, axis=1) - cnt
    counts = pref[-1]
    padded = (counts + tm_exp - 1) // tm_exp * tm_exp
    pad_ends = jnp.cumsum(padded)
    pad_starts = pad_ends - padded
    n_used = (pad_ends[-1:] // tm_exp).astype(i32)
    row0 = jnp.arange(n_blocks, dtype=i32) * tm_exp
    block_e = jnp.minimum(jnp.sum(row0[:, None] >= pad_ends[None, :], axis=1), N_EXP - 1).astype(i32)
    u0 = row0 - pad_starts[block_e]
    n_valid = jnp.where(row0 < pad_ends[-1], jnp.clip(counts[block_e] - u0, 0, tm_exp), 0).astype(i32)
    pref_e = pref[:, block_e].T
    tile_lo = jnp.sum(pref_e[:, 1:] <= u0[:, None], axis=1).astype(i32)
    tile_hi = jnp.sum(pref_e[:, :-1] < (u0 + tm_exp)[:, None], axis=1).astype(i32)
    expert_tables = (block_e, n_used, u0.astype(i32), n_valid, tile_lo, tile_hi,
                     pref.reshape(-1), loff.reshape(-1).astype(i32))
    src = pad_starts[None, :] + pref[:-1]
    combine_tables = (cnt.reshape(-1), src.reshape(-1).astype(i32), loff.reshape(-1).astype(i32))
    return expert_tables, combine_tables


def kernel(x, c, ada_w, ada_b, w_in, conv_dw_w, conv_dw_b, conv_ln_g, conv_ln_b, conv_pw_w,
           pool_w, pool_scale, sg_ln_g, sg_ln_b, sg_w, sg_b, mix_norm_g, w_out,
           router_w, router_b, moe_w1, moe_b1, moe_w2, moe_b2, final_g):
    B, S, d_model = x.shape
    L = ada_w.shape[0]
    assert d_model == D and w_in.shape[-1] == IN_COLS and moe_w1.shape[1] == N_EXP
    T = B * S
    tm_mix = _tile(S, 512)
    tm_tok = _tile(S, 512)
    tm_exp = _tile(S, 512)
    n_blocks = -(-(T * TOP_K) // tm_exp) + N_EXP

    w_in_b, w_out_b, pw_b, poolw_b = (w.astype(bf16) for w in (w_in, w_out, conv_pw_w, pool_w))
    w1_b, w2_b = moe_w1.astype(bf16), moe_w2.astype(bf16)
    rows3 = lambda a: a.reshape(L, 1, -1)
    dwb, clg, clb, pscale, slg, slb, mixg = (
        rows3(a) for a in (conv_dw_b, conv_ln_g, conv_ln_b, pool_scale, sg_ln_g, sg_ln_b, mix_norm_g))
    sgbt = jnp.swapaxes(sg_b, 1, 2)
    wrt = jnp.swapaxes(router_w, 1, 2)
    rb = router_b.reshape(L, N_EXP, 1)
    b1 = moe_b1.reshape(L, N_EXP, 1, 2 * F_EXP)
    b2 = moe_b2.reshape(L, N_EXP, 1, D)
    fg = final_g.reshape(1, D)

    mod = _ada_call(c, ada_w, ada_b)
    for l in range(L):
        sh1, sc1, g1, sh2, sc2, g2 = [m.reshape(B, 1, D) for m in jnp.split(mod[l], 6, axis=-1)]

        y = _mixer_call(l, x, sc1, sh1, w_in_b, conv_dw_w, dwb, clg, clb, pw_b, poolw_b, pscale,
                        slg, slb, sg_w, sgbt, mixg, tm_mix)
        x, hp, gates, lpos, tcnt = _outproj_router_call(l, x, y, w_out_b, g1, sc2, sh2, wrt, rb, tm_tok)

        expert_tables, combine_tables = _routing_tables(tcnt, tm_tok, tm_exp, n_blocks)
        xs = _permute_call(lpos, hp, tm_tok)
        ys = _expert_call(l, expert_tables, xs, w1_b, b1, w2_b, b2, n_blocks, tm_exp,
                          TOP_K * tm_tok)
        x = _combine_call(combine_tables, lpos, gates, x, g2, fg, ys, tm_tok, final=(l == L - 1))
    return x
```

```python
import functools

import jax
import jax.numpy as jnp
from jax import lax
from jax.experimental import pallas as pl
from jax.experimental.pallas import tpu as pltpu

f32 = jnp.float32
bf16 = jnp.bfloat16
u32 = jnp.uint32
i32 = jnp.int32

D = 2048
HEAD = 128
CONV_CH = 768
POOL_CH = 512
GATE_CH = 768
IN_COLS = 2 * CONV_CH + POOL_CH + 2 * GATE_CH
POOL_OFF = 2 * CONV_CH
GATE_OFF = 2 * CONV_CH + POOL_CH
POOL_WINDOWS = (2, 4, 8, 16)
CONV_W = 31
CHUNK = 128
N_EXP = 32
TOP_K = 4
F_EXP = 768
SWIGLU_ALPHA = 1.702
SWIGLU_LIMIT = 7.0
EPS = 1e-5

SUBLANES = 8
HALO = 32
CONV_ROWS = 64
PACK_W = D // 2
PACK_S = PACK_W // 128
assert PACK_S == SUBLANES
VMEM_LIMIT = 56 * 1024 * 1024
EXPERT_VMEM_LIMIT = 60 * 1024 * 1024


def _sigmoid(v):
    return 1.0 / (1.0 + jnp.exp(-v))


def _rms(v):
    return v * lax.rsqrt(jnp.mean(v * v, axis=-1, keepdims=True) + EPS)


def _layernorm(v, g, b):
    mu = jnp.mean(v, axis=-1, keepdims=True)
    d = v - mu
    var = jnp.mean(d * d, axis=-1, keepdims=True)
    return d * lax.rsqrt(var + EPS) * g + b


def _pack_pair(lo, hi):
    lo_bits = lax.bitcast_convert_type(lo.astype(bf16).astype(f32), u32)
    hi_bits = lax.bitcast_convert_type(hi.astype(bf16).astype(f32), u32)
    return (lo_bits >> 16) | (hi_bits & jnp.uint32(0xFFFF0000))


def _unpack_pair(w):
    lo = lax.bitcast_convert_type(w << 16, f32)
    hi = lax.bitcast_convert_type(w & jnp.uint32(0xFFFF0000), f32)
    return lo, hi


def _slab(n_tokens, s):
    return pl.ds(s, n_tokens, stride=PACK_S)


def _rows(first_row, n_rows):
    start = first_row * PACK_S
    if not isinstance(start, int):
        start = pl.multiple_of(start, PACK_S)
    return pl.ds(start, n_rows * PACK_S)


def _params(n_axes):
    return pltpu.CompilerParams(dimension_semantics=("arbitrary",) * n_axes,
                                vmem_limit_bytes=VMEM_LIMIT)


def _layer_resident(l, shape):
    nd = len(shape)
    return pl.BlockSpec((None,) + tuple(shape), lambda *_: (l,) + (0,) * nd,
                        pipeline_mode=pl.Buffered(1))


def _ada_kernel(c_ref, w_ref, b_ref, o_ref):
    c = c_ref[...]
    ca = (c * _sigmoid(c)).astype(bf16)
    o_ref[0] = jnp.dot(ca, w_ref[0].astype(bf16), preferred_element_type=f32) + b_ref[0]


def _ada_call(c, ada_w, ada_b):
    L, _, n6 = ada_w.shape
    B = c.shape[0]
    tn = 1024
    return pl.pallas_call(
        _ada_kernel,
        grid=(L, n6 // tn),
        in_specs=[pl.BlockSpec((B, D), lambda l, n: (0, 0)),
                  pl.BlockSpec((1, D, tn), lambda l, n: (l, 0, n)),
                  pl.BlockSpec((1, 1, tn), lambda l, n: (l, 0, n))],
        out_specs=pl.BlockSpec((1, B, tn), lambda l, n: (l, 0, n)),
        out_shape=jax.ShapeDtypeStruct((L, B, n6), f32),
        compiler_params=_params(2),
        name="ada_mod",
    )(c, ada_w, ada_b.reshape(L, 1, n6))


IN_PROJ_COLS = 512


def _mixer_kernel(x_ref, sc_ref, sh_ref, win_ref, dww_ref, dwb_ref, clg_ref, clb_ref, pw_ref,
                  poolw_ref, pscale_ref, slg_ref, slb_ref, sgw_ref, sgbt_ref, mixg_ref, y_ref,
                  znew, zbuf, abuf, pbuf, cbuf, shbuf, *, tiles_per_seq):
    step = pl.program_id(0)
    tm = y_ref.shape[1]
    j = (step + tiles_per_seq - 1) % tiles_per_seq

    @pl.when(step == 0)
    def _():
        zbuf[...] = jnp.zeros((tm, IN_COLS), bf16)

    @pl.when(j == 0)
    def _():
        abuf[0:HALO, :] = jnp.zeros((HALO, CONV_CH), f32)
        pbuf[0:HALO, :] = jnp.zeros((HALO, POOL_CH), f32)

    h = (_rms(x_ref[0]) * (1.0 + sc_ref[0]) + sh_ref[0]).astype(bf16)
    proj_chunks = list(range(0, IN_COLS, IN_PROJ_COLS))

    def project(n_chunks):
        for _ in range(min(n_chunks, len(proj_chunks))):
            n0 = proj_chunks.pop(0)
            znew[:, n0:n0 + IN_PROJ_COLS] = jnp.dot(
                h, win_ref[:, n0:n0 + IN_PROJ_COLS], preferred_element_type=f32).astype(bf16)

    za = zbuf[:, 0:CONV_CH].astype(f32)
    zg = zbuf[:, CONV_CH:2 * CONV_CH].astype(f32)
    abuf[HALO:HALO + tm, :] = za * _sigmoid(zg)
    first = HALO - (CONV_W - 1)
    span = CONV_ROWS + HALO
    for r0 in range(0, tm, CONV_ROWS):
        if (r0 // CONV_ROWS) % 2 == 0:
            project(1)
        for c0 in range(0, CONV_CH, 128):
            win = abuf[r0:r0 + span, c0:c0 + 128]
            for res in range(1, SUBLANES):
                shbuf[res - 1] = win[res:res + span - SUBLANES, :]
            acc = jnp.broadcast_to(dwb_ref[:, c0:c0 + 128], (CONV_ROWS, 128))
            for t in range(CONV_W):
                q, res = divmod(first + t, SUBLANES)
                rows = slice(SUBLANES * q, SUBLANES * q + CONV_ROWS)
                src = win[rows, :] if res == 0 else shbuf[res - 1, rows, :]
                acc = acc + dww_ref[t:t + 1, c0:c0 + 128] * src
            cbuf[r0:r0 + CONV_ROWS, c0:c0 + 128] = acc
    abuf[0:HALO, :] = abuf[tm:tm + HALO, :]
    project(1)
    an = _layernorm(cbuf[...], clg_ref[...], clb_ref[...])
    an = an * _sigmoid(an)
    ya = _rms(jnp.dot(an.astype(bf16), pw_ref[...], preferred_element_type=f32))
    y_ref[0, :, 0:CONV_CH] = (ya * mixg_ref[:, 0:CONV_CH]).astype(bf16)

    p = zbuf[:, POOL_OFF:POOL_OFF + POOL_CH].astype(f32)
    pbuf[HALO:HALO + tm, :] = p
    pos = (j * tm + 1 + lax.broadcasted_iota(i32, (tm, 1), 0)).astype(f32)
    ybs = []
    for g, w in enumerate(POOL_WINDOWS):
        c0 = g * HEAD
        win = pbuf[HALO:HALO + tm, c0:c0 + HEAD]
        for i in range(1, w):
            win = win + pbuf[HALO - i:HALO - i + tm, c0:c0 + HEAD]
        diff = win / jnp.minimum(pos, float(w)) - p[:, c0:c0 + HEAD]
        yg = jnp.dot(diff.astype(bf16), poolw_ref[g], preferred_element_type=f32)
        ybs.append(yg * pscale_ref[:, c0:c0 + HEAD])
    pbuf[0:HALO, :] = pbuf[tm:tm + HALO, :]
    project(1)
    yb = _rms(jnp.concatenate(ybs, axis=-1))
    y_ref[0, :, CONV_CH:CONV_CH + POOL_CH] = (yb * mixg_ref[:, CONV_CH:CONV_CH + POOL_CH]).astype(bf16)

    zc = zbuf[:, GATE_OFF:GATE_OFF + 2 * GATE_CH].astype(f32)
    gl = 0.5 * zc * (1.0 + jnp.tanh(0.7978845608028654 * (zc + 0.044715 * zc * zc * zc)))
    u = gl[:, 0:GATE_CH]
    vn = _layernorm(gl[:, GATE_CH:], slg_ref[...], slb_ref[...]).astype(bf16)
    causal = (lax.broadcasted_iota(i32, (CHUNK, CHUNK), 1)
              <= lax.broadcasted_iota(i32, (CHUNK, CHUNK), 0))
    for hh in range(GATE_CH // HEAD):
        c0 = hh * HEAD
        wm = jnp.where(causal, sgw_ref[hh], 0.0).astype(bf16)
        bias = sgbt_ref[:, hh:hh + 1]
        for n in range(tm // CHUNK):
            r0 = n * CHUNK
            s = jnp.dot(wm, vn[r0:r0 + CHUNK, c0:c0 + HEAD], preferred_element_type=f32) + bias
            cbuf[r0:r0 + CHUNK, c0:c0 + HEAD] = u[r0:r0 + CHUNK, c0:c0 + HEAD] * s
    project(len(proj_chunks))
    yc = _rms(cbuf[...])
    y_ref[0, :, CONV_CH + POOL_CH:] = (yc * mixg_ref[:, CONV_CH + POOL_CH:]).astype(bf16)

    zbuf[...] = znew[...]


def _mixer_call(l, x, sc, sh, w_in_b, dww, dwb, clg, clb, pw_b, poolw_b, pscale, slg, slb, sgw, sgbt,
                mixg, tm):
    B, S, _ = x.shape
    nj = S // tm
    n_tiles = B * nj
    n_heads = GATE_CH // HEAD
    lr = functools.partial(_layer_resident, l)
    proj = lambda s: jnp.minimum(s, n_tiles - 1)
    mixed = lambda s: jnp.maximum(s - 1, 0)
    return pl.pallas_call(
        functools.partial(_mixer_kernel, tiles_per_seq=nj),
        grid=(n_tiles + 1,),
        in_specs=[pl.BlockSpec((1, tm, D), lambda s: (proj(s) // nj, proj(s) % nj, 0)),
                  pl.BlockSpec((1, 1, D), lambda s: (proj(s) // nj, 0, 0)),
                  pl.BlockSpec((1, 1, D), lambda s: (proj(s) // nj, 0, 0)),
                  lr((D, IN_COLS)),
                  lr((CONV_W, CONV_CH)), lr((1, CONV_CH)), lr((1, CONV_CH)), lr((1, CONV_CH)),
                  lr((CONV_CH, CONV_CH)),
                  lr((len(POOL_WINDOWS), HEAD, HEAD)), lr((1, POOL_CH)),
                  lr((1, GATE_CH)), lr((1, GATE_CH)),
                  lr((n_heads, CHUNK, CHUNK)), lr((CHUNK, n_heads)),
                  lr((1, D))],
        out_specs=pl.BlockSpec((1, tm, D), lambda s: (mixed(s) // nj, mixed(s) % nj, 0)),
        out_shape=jax.ShapeDtypeStruct((B, S, D), bf16),
        scratch_shapes=[pltpu.VMEM((tm, IN_COLS), bf16),
                        pltpu.VMEM((tm, IN_COLS), bf16),
                        pltpu.VMEM((HALO + tm, CONV_CH), f32),
                        pltpu.VMEM((HALO + tm, POOL_CH), f32),
                        pltpu.VMEM((tm, CONV_CH), f32),
                        pltpu.VMEM((SUBLANES - 1, CONV_ROWS + HALO - SUBLANES, 128), f32)],
        compiler_params=_params(1),
        name="in_proj_mixers",
    )(x, sc, sh, w_in_b, dww, dwb, clg, clb, pw_b, poolw_b, pscale, slg, slb, sgw, sgbt, mixg)


def _outproj_router_kernel(x_ref, y_ref, wout_ref, g1_ref, sc_ref, sh_ref, wrt_ref, rb_ref,
                           xo_ref, hp_ref, gate_ref, lpos_ref, tcnt_ref):
    tm = x_ref.shape[1]
    x = x_ref[0] + g1_ref[0] * jnp.dot(y_ref[0], wout_ref[...], preferred_element_type=f32)
    xo_ref[0] = x
    h = _rms(x) * (1.0 + sc_ref[0]) + sh_ref[0]
    h_hi = h.astype(bf16)
    h_hi32 = h_hi.astype(f32)
    h_lo = (h - h_hi32).astype(bf16)
    for s in range(PACK_S):
        hp_ref[_slab(tm, s), :] = _pack_pair(h_hi32[:, 256 * s:256 * s + 128],
                                             h_hi32[:, 256 * s + 128:256 * s + 256])

    wr = wrt_ref[...]
    w_hi = wr.astype(bf16)
    w_lo = (wr - w_hi.astype(f32)).astype(bf16)
    nt = (((1,), (1,)), ((), ()))
    lg = (lax.dot_general(w_hi, h_hi, nt, preferred_element_type=f32)
          + lax.dot_general(w_hi, h_lo, nt, preferred_element_type=f32)
          + lax.dot_general(w_lo, h_hi, nt, preferred_element_type=f32)
          + rb_ref[...])

    e_iota = lax.broadcasted_iota(i32, (N_EXP, tm), 0)
    vals, hots = [], []
    for _ in range(TOP_K):
        m = jnp.max(lg, axis=0, keepdims=True)
        idx = jnp.min(jnp.where(lg == m, e_iota, N_EXP), axis=0, keepdims=True)
        hot = e_iota == idx
        lg = jnp.where(hot, -jnp.inf, lg)
        vals.append(m)
        hots.append(hot)
    exps = [jnp.exp(v - vals[0]) for v in vals]
    denom = exps[0] + exps[1] + exps[2] + exps[3]
    gate_ref[...] = jnp.concatenate([e / denom for e in exps], axis=0)

    sel = jnp.zeros((N_EXP, tm), f32)
    for hot in hots:
        sel = sel + jnp.where(hot, 1.0, 0.0)
    sel_b = sel.astype(bf16)
    upper = jnp.where(lax.broadcasted_iota(i32, (tm, tm), 0) <= lax.broadcasted_iota(i32, (tm, tm), 1),
                      1.0, 0.0).astype(bf16)
    cum = jnp.dot(sel_b, upper, preferred_element_type=f32)
    below = jnp.where(lax.broadcasted_iota(i32, (N_EXP, N_EXP), 1)
                      < lax.broadcasted_iota(i32, (N_EXP, N_EXP), 0), 1.0, 0.0).astype(bf16)
    lower = jnp.sum(jnp.dot(below, sel_b, preferred_element_type=f32), axis=1, keepdims=True)
    place = lower + cum - sel
    lpos = [jnp.sum(jnp.where(hot, place, 0.0), axis=0, keepdims=True) for hot in hots]
    lpos_ref[...] = jnp.concatenate(lpos, axis=0).astype(i32) * PACK_S
    tcnt_ref[...] = jnp.broadcast_to(cum[:, tm - 1:tm], (N_EXP, 128))


def _outproj_router_call(l, x, y, w_out_b, g1, sc2, sh2, wrt, rb, tm):
    B, S, _ = x.shape
    T = B * S
    nj = S // tm
    tok = lambda b, j: (0, b * nj + j)
    return pl.pallas_call(
        _outproj_router_kernel,
        grid=(B, nj),
        in_specs=[pl.BlockSpec((1, tm, D), lambda b, j: (b, j, 0)),
                  pl.BlockSpec((1, tm, D), lambda b, j: (b, j, 0)),
                  _layer_resident(l, (D, D)),
                  pl.BlockSpec((1, 1, D), lambda b, j: (b, 0, 0)),
                  pl.BlockSpec((1, 1, D), lambda b, j: (b, 0, 0)),
                  pl.BlockSpec((1, 1, D), lambda b, j: (b, 0, 0)),
                  _layer_resident(l, (N_EXP, D)), _layer_resident(l, (N_EXP, 1))],
        out_specs=[pl.BlockSpec((1, tm, D), lambda b, j: (b, j, 0)),
                   pl.BlockSpec((tm * PACK_S, 128), lambda b, j: (b * nj + j, 0)),
                   pl.BlockSpec((TOP_K, tm), tok),
                   pl.BlockSpec((TOP_K, tm), tok),
                   pl.BlockSpec((None, N_EXP, 128), lambda b, j: (b * nj + j, 0, 0))],
        out_shape=[jax.ShapeDtypeStruct((B, S, D), f32),
                   jax.ShapeDtypeStruct((T * PACK_S, 128), u32),
                   jax.ShapeDtypeStruct((TOP_K, T), f32),
                   jax.ShapeDtypeStruct((TOP_K, T), i32),
                   jax.ShapeDtypeStruct((T // tm, N_EXP, 128), f32)],
        compiler_params=_params(2),
        name="out_proj_router",
    )(x, y, w_out_b, g1, sc2, sh2, wrt, rb)


def _permute_kernel(lpos_ref, h_ref, xs_ref):
    tm = h_ref.shape[0] // PACK_S
    unroll = 4

    def body(i, carry):
        for j in range(unroll):
            t = i * unroll + j
            v = h_ref[pl.ds(pl.multiple_of(t * PACK_S, PACK_S), PACK_S), :]
            for k in range(TOP_K):
                dst = pl.multiple_of(lpos_ref[k, t], PACK_S)
                xs_ref[pl.ds(dst, PACK_S), :] = v
        return carry

    lax.fori_loop(0, tm // unroll, body, 0)


def _permute_call(lpos_tiles, hp, tm):
    T = hp.shape[0] // PACK_S
    return pl.pallas_call(
        _permute_kernel,
        grid=(T // tm,),
        in_specs=[pl.BlockSpec((TOP_K, tm), lambda i: (0, i), memory_space=pltpu.SMEM),
                  pl.BlockSpec((tm * PACK_S, 128), lambda i: (i, 0))],
        out_specs=pl.BlockSpec((TOP_K * tm * PACK_S, 128), lambda i: (i, 0)),
        out_shape=jax.ShapeDtypeStruct((TOP_K * T * PACK_S, 128), u32),
        compiler_params=_params(1),
        name="tile_permute",
    )(lpos_tiles, hp)


def _expert_kernel(be_ref, nu_ref, u0_ref, nv_ref, ilo_ref, ihi_ref, pref_ref, loff_ref,
                   xs_ref, w1f_ref, b1_ref, w2f_ref, b2_ref, ys_ref, xin, xbuf, w1_ref, w2_ref, sems,
                   *, tile_rows):
    blk = pl.program_id(0)
    tm = xbuf.shape[0]
    n_used = nu_ref[0]

    def fetch(b, slot):
        e = be_ref[b]
        u0 = u0_ref[b]

        def seg(i, carry):
            s0 = pref_ref[i * N_EXP + e]
            s1 = pref_ref[(i + 1) * N_EXP + e]
            a = jnp.maximum(s0, u0)
            n = jnp.minimum(s1, u0 + tm) - a

            @pl.when(n > 0)
            def _():
                src = i * tile_rows + loff_ref[i * N_EXP + e] + (a - s0)
                pltpu.make_async_copy(xs_ref.at[_rows(src, n)], xin.at[slot, _rows(a - u0, n)],
                                      sems.at[slot]).start()
            return carry

        lax.fori_loop(ilo_ref[b], ihi_ref[b], seg, 0)

    @pl.when((blk == 0) & (n_used > 0))
    def _():
        fetch(0, 0)

    @pl.when(blk + 1 < n_used)
    def _():
        fetch(blk + 1, (blk + 1) % 2)

    @pl.when((blk < n_used) & ((blk == 0) | (be_ref[blk] != be_ref[jnp.maximum(blk - 1, 0)])))
    def _():
        w1_ref[...] = w1f_ref[...].astype(bf16)
        w2_ref[...] = w2f_ref[...].astype(bf16)

    @pl.when(blk < n_used)
    def _():
        slot = blk % 2
        n_valid = nv_ref[blk]
        pltpu.make_async_copy(xs_ref.at[_rows(0, n_valid)], xin.at[slot, _rows(0, n_valid)],
                              sems.at[slot]).wait()

        def clear(r, carry):
            xin[slot, _rows(r, 1), :] = jnp.zeros((PACK_S, 128), u32)
            return carry

        lax.fori_loop(n_valid, tm, clear, 0)
        for s in range(PACK_S):
            lo, hi = _unpack_pair(xin[slot, _slab(tm, s), :])
            xbuf[:, 256 * s:256 * s + 128] = lo.astype(bf16)
            xbuf[:, 256 * s + 128:256 * s + 256] = hi.astype(bf16)
        z = jnp.dot(xbuf[...], w1_ref[...], preferred_element_type=f32) + b1_ref[...]
        glu = jnp.minimum(z[:, 0:F_EXP], SWIGLU_LIMIT)
        lin = jnp.clip(z[:, F_EXP:], -SWIGLU_LIMIT, SWIGLU_LIMIT)
        a = glu * _sigmoid(SWIGLU_ALPHA * glu) * (lin + 1.0)
        o = jnp.dot(a.astype(bf16), w2_ref[...], preferred_element_type=f32) + b2_ref[...]
        for s in range(PACK_S):
            ys_ref[_slab(tm, s), :] = _pack_pair(o[:, 256 * s:256 * s + 128],
                                                 o[:, 256 * s + 128:256 * s + 256])

    @pl.when(blk >= n_used)
    def _():
        ys_ref[...] = jnp.zeros_like(ys_ref)


def _expert_call(l, tables, xs, w1, b1, w2, b2, n_blocks, tm, tile_rows):
    n_pref = len(tables)
    per_expert = lambda shape: pl.BlockSpec((None, None) + shape,
                                            lambda i, be, *_: (l, be[i]) + (0,) * len(shape))
    return pl.pallas_call(
        functools.partial(_expert_kernel, tile_rows=tile_rows),
        grid_spec=pltpu.PrefetchScalarGridSpec(
            num_scalar_prefetch=n_pref,
            grid=(n_blocks,),
            in_specs=[pl.BlockSpec(memory_space=pl.ANY),
                      per_expert((D, 2 * F_EXP)), per_expert((1, 2 * F_EXP)),
                      per_expert((F_EXP, D)), per_expert((1, D))],
            out_specs=pl.BlockSpec((tm * PACK_S, 128), lambda i, *_: (i, 0)),
            scratch_shapes=[pltpu.VMEM((2, tm * PACK_S, 128), u32),
                            pltpu.VMEM((tm, D), bf16),
                            pltpu.VMEM((D, 2 * F_EXP), bf16),
                            pltpu.VMEM((F_EXP, D), bf16),
                            pltpu.SemaphoreType.DMA((2,))]),
        out_shape=jax.ShapeDtypeStruct((n_blocks * tm * PACK_S, 128), u32),
        compiler_params=pltpu.CompilerParams(dimension_semantics=("arbitrary",),
                                             vmem_limit_bytes=EXPERT_VMEM_LIMIT),
        name="expert_ffn",
    )(*tables, xs, w1, b1, w2, b2)


def _combine_kernel(cnt_ref, src_ref, loff_ref, lpos_ref, gate_ref, x_ref, g2_ref, fg_ref, ys_ref,
                    o_ref, gbuf, flo, fhi, sems, *, final):
    tm = x_ref.shape[1]
    step = pl.program_id(0) * pl.num_programs(1) + pl.program_id(1)
    n_steps = pl.num_programs(0) * pl.num_programs(1)

    def fetch(tile, slot):
        for e in range(N_EXP):
            n = cnt_ref[tile * N_EXP + e]

            @pl.when(n > 0)
            def _():
                pltpu.make_async_copy(ys_ref.at[_rows(src_ref[tile * N_EXP + e], n)],
                                      gbuf.at[slot, _rows(loff_ref[tile * N_EXP + e], n)],
                                      sems.at[slot]).start()

    @pl.when(step == 0)
    def _():
        fetch(0, 0)

    for slot in range(2):
        @pl.when((step + 1 < n_steps) & ((step + 1) % 2 == slot))
        def _():
            fetch(step + 1, slot)

    unroll = 2
    for slot in range(2):
        @pl.when(step % 2 == slot)
        def _():
            pltpu.make_async_copy(ys_ref.at[_rows(0, TOP_K * tm)], gbuf.at[slot], sems.at[slot]).wait()

            def body(i, carry):
                for j in range(unroll):
                    t = i * unroll + j
                    acc_lo = jnp.zeros((PACK_S, 128), f32)
                    acc_hi = jnp.zeros((PACK_S, 128), f32)
                    for k in range(TOP_K):
                        src = pl.multiple_of(lpos_ref[k, t], PACK_S)
                        lo, hi = _unpack_pair(gbuf[slot, pl.ds(src, PACK_S), :])
                        g = gate_ref[k, t]
                        acc_lo = acc_lo + g * lo
                        acc_hi = acc_hi + g * hi
                    flo[_rows(t, 1), :] = acc_lo
                    fhi[_rows(t, 1), :] = acc_hi
                return carry

            lax.fori_loop(0, tm // unroll, body, 0)

    for s in range(PACK_S):
        for half, fref in enumerate((flo, fhi)):
            cols = slice(256 * s + 128 * half, 256 * s + 128 * half + 128)
            o_ref[0, :, cols] = x_ref[0, :, cols] + g2_ref[0, :, cols] * fref[_slab(tm, s), :]
    if final:
        o_ref[0] = _rms(o_ref[0]) * fg_ref[...]


def _combine_call(tables, lpos_tiles, gate_tiles, x, g2, final_g, ys, tm, final):
    B, S, _ = x.shape
    nj = S // tm
    smem_tile = lambda: pl.BlockSpec((TOP_K, tm), lambda b, j, *_: (0, b * nj + j),
                                     memory_space=pltpu.SMEM)
    return pl.pallas_call(
        functools.partial(_combine_kernel, final=final),
        grid_spec=pltpu.PrefetchScalarGridSpec(
            num_scalar_prefetch=len(tables),
            grid=(B, nj),
            in_specs=[smem_tile(), smem_tile(),
                      pl.BlockSpec((1, tm, D), lambda b, j, *_: (b, j, 0)),
                      pl.BlockSpec((1, 1, D), lambda b, j, *_: (b, 0, 0)),
                      pl.BlockSpec((1, D), lambda b, j, *_: (0, 0)),
                      pl.BlockSpec(memory_space=pl.ANY)],
            out_specs=pl.BlockSpec((1, tm, D), lambda b, j, *_: (b, j, 0)),
            scratch_shapes=[pltpu.VMEM((2, TOP_K * tm * PACK_S, 128), u32),
                            pltpu.VMEM((tm * PACK_S, 128), f32),
                            pltpu.VMEM((tm * PACK_S, 128), f32),
                            pltpu.SemaphoreType.DMA((2,))]),
        out_shape=jax.ShapeDtypeStruct((B, S, D), f32),
        compiler_params=_params(2),
        name="combine",
    )(*tables, lpos_tiles, gate_tiles, x, g2, final_g, ys)


def _tile(n, want):
    t = min(n, want)
    assert n % t == 0 and t % CHUNK == 0, (n, t)
    return t


def _routing_tables(tcnt, tm_tok, tm_exp, n_blocks):
    n_tiles = tcnt.shape[0]
    cnt = tcnt[:, :, 0].astype(i32)
    pref = jnp.concatenate([jnp.zeros((1, N_EXP), i32), jnp.cumsum(cnt, axis=0)])
    loff = jnp.cumsum(cnt, axis=1) - cnt
    counts = pref[-1]
    padded = (counts + tm_exp - 1) // tm_exp * tm_exp
    pad_ends = jnp.cumsum(padded)
    pad_starts = pad_ends - padded
    n_used = (pad_ends[-1:] // tm_exp).astype(i32)
    row0 = jnp.arange(n_blocks, dtype=i32) * tm_exp
    block_e = jnp.minimum(jnp.sum(row0[:, None] >= pad_ends[None, :], axis=1), N_EXP - 1).astype(i32)
    u0 = row0 - pad_starts[block_e]
    n_valid = jnp.where(row0 < pad_ends[-1], jnp.clip(counts[block_e] - u0, 0, tm_exp), 0).astype(i32)
    pref_e = pref[:, block_e].T
    tile_lo = jnp.sum(pref_e[:, 1:] <= u0[:, None], axis=1).astype(i32)
    tile_hi = jnp.sum(pref_e[:, :-1] < (u0 + tm_exp)[:, None], axis=1).astype(i32)
    expert_tables = (block_e, n_used, u0.astype(i32), n_valid, tile_lo, tile_hi,
                     pref.reshape(-1), loff.reshape(-1).astype(i32))
    src = pad_starts[None, :] + pref[:-1]
    combine_tables = (cnt.reshape(-1), src.reshape(-1).astype(i32), loff.reshape(-1).astype(i32))
    return expert_tables, combine_tables


def kernel(x, c, ada_w, ada_b, w_in, conv_dw_w, conv_dw_b, conv_ln_g, conv_ln_b, conv_pw_w,
           pool_w, pool_scale, sg_ln_g, sg_ln_b, sg_w, sg_b, mix_norm_g, w_out,
           router_w, router_b, moe_w1, moe_b1, moe_w2, moe_b2, final_g):
    B, S, d_model = x.shape
    L = ada_w.shape[0]
    assert d_model == D and w_in.shape[-1] == IN_COLS and moe_w1.shape[1] == N_EXP
    T = B * S
    tm_mix = _tile(S, 512)
    tm_tok = _tile(S, 512)
    tm_exp = _tile(S, 512)
    n_blocks = -(-(T * TOP_K) // tm_exp) + N_EXP

    w_in_b, w_out_b, pw_b, poolw_b = (w.astype(bf16) for w in (w_in, w_out, conv_pw_w, pool_w))
    rows3 = lambda a: a.reshape(L, 1, -1)
    dwb, clg, clb, pscale, slg, slb, mixg = (
        rows3(a) for a in (conv_dw_b, conv_ln_g, conv_ln_b, pool_scale, sg_ln_g, sg_ln_b, mix_norm_g))
    sgbt = jnp.swapaxes(sg_b, 1, 2)
    wrt = jnp.swapaxes(router_w, 1, 2)
    rb = router_b.reshape(L, N_EXP, 1)
    b1 = moe_b1.reshape(L, N_EXP, 1, 2 * F_EXP)
    b2 = moe_b2.reshape(L, N_EXP, 1, D)
    fg = final_g.reshape(1, D)

    mod = _ada_call(c, ada_w, ada_b)
    for l in range(L):
        sh1, sc1, g1, sh2, sc2, g2 = [m.reshape(B, 1, D) for m in jnp.split(mod[l], 6, axis=-1)]

        y = _mixer_call(l, x, sc1, sh1, w_in_b, conv_dw_w, dwb, clg, clb, pw_b, poolw_b, pscale,
                        slg, slb, sg_w, sgbt, mixg, tm_mix)
        x, hp, gates, lpos, tcnt = _outproj_router_call(l, x, y, w_out_b, g1, sc2, sh2, wrt, rb, tm_tok)

        expert_tables, combine_tables = _routing_tables(tcnt, tm_tok, tm_exp, n_blocks)
        xs = _permute_call(lpos, hp, tm_tok)
        ys = _expert_call(l, expert_tables, xs, moe_w1, b1, moe_w2, b2, n_blocks, tm_exp,
                          TOP_K * tm_tok)
        x = _combine_call(combine_tables, lpos, gates, x, g2, fg, ys, tm_tok, final=(l == L - 1))
    return x
```

```python
import functools

import jax
import jax.numpy as jnp
from jax import lax
from jax.experimental import pallas as pl
from jax.experimental.pallas import tpu as pltpu

f32 = jnp.float32
bf16 = jnp.bfloat16
u32 = jnp.uint32
i32 = jnp.int32

D = 2048
HEAD = 128
CONV_CH = 768
POOL_CH = 512
GATE_CH = 768
IN_COLS = 2 * CONV_CH + POOL_CH + 2 * GATE_CH
POOL_OFF = 2 * CONV_CH
GATE_OFF = 2 * CONV_CH + POOL_CH
POOL_WINDOWS = (2, 4, 8, 16)
CONV_W = 31
CHUNK = 128
N_EXP = 32
TOP_K = 4
F_EXP = 768
SWIGLU_ALPHA = 1.702
SWIGLU_LIMIT = 7.0
EPS = 1e-5

SUBLANES = 8
HALO = 32
CONV_ROWS = 64
PACK_W = D // 2
PACK_S = PACK_W // 128
assert PACK_S == SUBLANES
VMEM_LIMIT = 56 * 1024 * 1024
EXPERT_VMEM_LIMIT = 60 * 1024 * 1024


def _sigmoid(v):
    return 1.0 / (1.0 + jnp.exp(-v))


def _rms(v):
    return v * lax.rsqrt(jnp.mean(v * v, axis=-1, keepdims=True) + EPS)


def _layernorm(v, g, b):
    mu = jnp.mean(v, axis=-1, keepdims=True)
    d = v - mu
    var = jnp.mean(d * d, axis=-1, keepdims=True)
    return d * lax.rsqrt(var + EPS) * g + b


def _pack_pair(lo, hi):
    lo_bits = lax.bitcast_convert_type(lo.astype(bf16).astype(f32), u32)
    hi_bits = lax.bitcast_convert_type(hi.astype(bf16).astype(f32), u32)
    return (lo_bits >> 16) | (hi_bits & jnp.uint32(0xFFFF0000))


def _unpack_pair(w):
    lo = lax.bitcast_convert_type(w << 16, f32)
    hi = lax.bitcast_convert_type(w & jnp.uint32(0xFFFF0000), f32)
    return lo, hi


def _slab(n_tokens, s):
    return pl.ds(s, n_tokens, stride=PACK_S)


def _rows(first_row, n_rows):
    start = first_row * PACK_S
    if not isinstance(start, int):
        start = pl.multiple_of(start, PACK_S)
    return pl.ds(start, n_rows * PACK_S)


def _params(n_axes):
    return pltpu.CompilerParams(dimension_semantics=("arbitrary",) * n_axes,
                                vmem_limit_bytes=VMEM_LIMIT)


def _layer_resident(l, shape):
    nd = len(shape)
    return pl.BlockSpec((None,) + tuple(shape), lambda *_: (l,) + (0,) * nd,
                        pipeline_mode=pl.Buffered(1))


def _ada_kernel(c_ref, w_ref, b_ref, o_ref):
    c = c_ref[...]
    ca = (c * _sigmoid(c)).astype(bf16)
    o_ref[0] = jnp.dot(ca, w_ref[0].astype(bf16), preferred_element_type=f32) + b_ref[0]


def _ada_call(c, ada_w, ada_b):
    L, _, n6 = ada_w.shape
    B = c.shape[0]
    tn = 1024
    return pl.pallas_call(
        _ada_kernel,
        grid=(L, n6 // tn),
        in_specs=[pl.BlockSpec((B, D), lambda l, n: (0, 0)),
                  pl.BlockSpec((1, D, tn), lambda l, n: (l, 0, n)),
                  pl.BlockSpec((1, 1, tn), lambda l, n: (l, 0, n))],
        out_specs=pl.BlockSpec((1, B, tn), lambda l, n: (l, 0, n)),
        out_shape=jax.ShapeDtypeStruct((L, B, n6), f32),
        compiler_params=_params(2),
        name="ada_mod",
    )(c, ada_w, ada_b.reshape(L, 1, n6))


IN_PROJ_COLS = 512


def _mixer_kernel(x_ref, sc_ref, sh_ref, win_ref, dww_ref, dwb_ref, clg_ref, clb_ref, pw_ref,
                  poolw_ref, pscale_ref, slg_ref, slb_ref, sgw_ref, sgbt_ref, mixg_ref, y_ref,
                  znew, zbuf, abuf, pbuf, cbuf, shbuf, *, tiles_per_seq):
    step = pl.program_id(0)
    tm = y_ref.shape[1]
    j = (step + tiles_per_seq - 1) % tiles_per_seq

    @pl.when(step == 0)
    def _():
        zbuf[...] = jnp.zeros((tm, IN_COLS), bf16)

    @pl.when(j == 0)
    def _():
        abuf[0:HALO, :] = jnp.zeros((HALO, CONV_CH), f32)
        pbuf[0:HALO, :] = jnp.zeros((HALO, POOL_CH), f32)

    h = (_rms(x_ref[0]) * (1.0 + sc_ref[0]) + sh_ref[0]).astype(bf16)
    proj_chunks = list(range(0, IN_COLS, IN_PROJ_COLS))

    def project(n_chunks):
        for _ in range(min(n_chunks, len(proj_chunks))):
            n0 = proj_chunks.pop(0)
            znew[:, n0:n0 + IN_PROJ_COLS] = jnp.dot(
                h, win_ref[:, n0:n0 + IN_PROJ_COLS], preferred_element_type=f32).astype(bf16)

    za = zbuf[:, 0:CONV_CH].astype(f32)
    zg = zbuf[:, CONV_CH:2 * CONV_CH].astype(f32)
    abuf[HALO:HALO + tm, :] = za * _sigmoid(zg)
    first = HALO - (CONV_W - 1)
    span = CONV_ROWS + HALO
    for r0 in range(0, tm, CONV_ROWS):
        if (r0 // CONV_ROWS) % 2 == 0:
            project(1)
        for c0 in range(0, CONV_CH, 128):
            win = abuf[r0:r0 + span, c0:c0 + 128]
            for res in range(1, SUBLANES):
                shbuf[res - 1] = win[res:res + span - SUBLANES, :]
            acc = jnp.broadcast_to(dwb_ref[:, c0:c0 + 128], (CONV_ROWS, 128))
            for t in range(CONV_W):
                q, res = divmod(first + t, SUBLANES)
                rows = slice(SUBLANES * q, SUBLANES * q + CONV_ROWS)
                src = win[rows, :] if res == 0 else shbuf[res - 1, rows, :]
                acc = acc + dww_ref[t:t + 1, c0:c0 + 128] * src
            cbuf[r0:r0 + CONV_ROWS, c0:c0 + 128] = acc
    abuf[0:HALO, :] = abuf[tm:tm + HALO, :]
    project(1)
    an = _layernorm(cbuf[...], clg_ref[...], clb_ref[...])
    an = an * _sigmoid(an)
    ya = _rms(jnp.dot(an.astype(bf16), pw_ref[...], preferred_element_type=f32))
    y_ref[0, :, 0:CONV_CH] = (ya * mixg_ref[:, 0:CONV_CH]).astype(bf16)

    p = zbuf[:, POOL_OFF:POOL_OFF + POOL_CH].astype(f32)
    pbuf[HALO:HALO + tm, :] = p
    pos = (j * tm + 1 + lax.broadcasted_iota(i32, (tm, 1), 0)).astype(f32)
    ybs = []
    for g, w in enumerate(POOL_WINDOWS):
        c0 = g * HEAD
        win = pbuf[HALO:HALO + tm, c0:c0 + HEAD]
        for i in range(1, w):
            win = win + pbuf[HALO - i:HALO - i + tm, c0:c0 + HEAD]
        diff = win / jnp.minimum(pos, float(w)) - p[:, c0:c0 + HEAD]
        yg = jnp.dot(diff.astype(bf16), poolw_ref[g], preferred_element_type=f32)
        ybs.append(yg * pscale_ref[:, c0:c0 + HEAD])
    pbuf[0:HALO, :] = pbuf[tm:tm + HALO, :]
    project(1)
    yb = _rms(jnp.concatenate(ybs, axis=-1))
    y_ref[0, :, CONV_CH:CONV_CH + POOL_CH] = (yb * mixg_ref[:, CONV_CH:CONV_CH + POOL_CH]).astype(bf16)

    zc = zbuf[:, GATE_OFF:GATE_OFF + 2 * GATE_CH].astype(f32)
    gl = 0.5 * zc * (1.0 + jnp.tanh(0.7978845608028654 * (zc + 0.044715 * zc * zc * zc)))
    u = gl[:, 0:GATE_CH]
    vn = _layernorm(gl[:, GATE_CH:], slg_ref[...], slb_ref[...]).astype(bf16)
    causal = (lax.broadcasted_iota(i32, (CHUNK, CHUNK), 1)
              <= lax.broadcasted_iota(i32, (CHUNK, CHUNK), 0))
    for hh in range(GATE_CH // HEAD):
        c0 = hh * HEAD
        wm = jnp.where(causal, sgw_ref[hh], 0.0).astype(bf16)
        bias = sgbt_ref[:, hh:hh + 1]
        for n in range(tm // CHUNK):
            r0 = n * CHUNK
            s = jnp.dot(wm, vn[r0:r0 + CHUNK, c0:c0 + HEAD], preferred_element_type=f32) + bias
            cbuf[r0:r0 + CHUNK, c0:c0 + HEAD] = u[r0:r0 + CHUNK, c0:c0 + HEAD] * s
    project(len(proj_chunks))
    yc = _rms(cbuf[...])
    y_ref[0, :, CONV_CH + POOL_CH:] = (yc * mixg_ref[:, CONV_CH + POOL_CH:]).astype(bf16)

    zbuf[...] = znew[...]


def _mixer_call(l, x, sc, sh, w_in_b, dww, dwb, clg, clb, pw_b, poolw_b, pscale, slg, slb, sgw, sgbt,
                mixg, tm):
    B, S, _ = x.shape
    nj = S // tm
    n_tiles = B * nj
    n_heads = GATE_CH // HEAD
    lr = functools.partial(_layer_resident, l)
    proj = lambda s: jnp.minimum(s, n_tiles - 1)
    mixed = lambda s: jnp.maximum(s - 1, 0)
    return pl.pallas_call(
        functools.partial(_mixer_kernel, tiles_per_seq=nj),
        grid=(n_tiles + 1,),
        in_specs=[pl.BlockSpec((1, tm, D), lambda s: (proj(s) // nj, proj(s) % nj, 0)),
                  pl.BlockSpec((1, 1, D), lambda s: (proj(s) // nj, 0, 0)),
                  pl.BlockSpec((1, 1, D), lambda s: (proj(s) // nj, 0, 0)),
                  lr((D, IN_COLS)),
                  lr((CONV_W, CONV_CH)), lr((1, CONV_CH)), lr((1, CONV_CH)), lr((1, CONV_CH)),
                  lr((CONV_CH, CONV_CH)),
                  lr((len(POOL_WINDOWS), HEAD, HEAD)), lr((1, POOL_CH)),
                  lr((1, GATE_CH)), lr((1, GATE_CH)),
                  lr((n_heads, CHUNK, CHUNK)), lr((CHUNK, n_heads)),
                  lr((1, D))],
        out_specs=pl.BlockSpec((1, tm, D), lambda s: (mixed(s) // nj, mixed(s) % nj, 0)),
        out_shape=jax.ShapeDtypeStruct((B, S, D), bf16),
        scratch_shapes=[pltpu.VMEM((tm, IN_COLS), bf16),
                        pltpu.VMEM((tm, IN_COLS), bf16),
                        pltpu.VMEM((HALO + tm, CONV_CH), f32),
                        pltpu.VMEM((HALO + tm, POOL_CH), f32),
                        pltpu.VMEM((tm, CONV_CH), f32),
                        pltpu.VMEM((SUBLANES - 1, CONV_ROWS + HALO - SUBLANES, 128), f32)],
        compiler_params=_params(1),
        name="in_proj_mixers",
    )(x, sc, sh, w_in_b, dww, dwb, clg, clb, pw_b, poolw_b, pscale, slg, slb, sgw, sgbt, mixg)


def _outproj_router_kernel(x_ref, y_ref, wout_ref, g1_ref, sc_ref, sh_ref, wrt_ref, rb_ref,
                           xo_ref, hp_ref, gate_ref, lpos_ref, tcnt_ref):
    tm = x_ref.shape[1]
    x = x_ref[0] + g1_ref[0] * jnp.dot(y_ref[0], wout_ref[...], preferred_element_type=f32)
    xo_ref[0] = x
    h = _rms(x) * (1.0 + sc_ref[0]) + sh_ref[0]
    h_hi = h.astype(bf16)
    h_hi32 = h_hi.astype(f32)
    h_lo = (h - h_hi32).astype(bf16)
    for s in range(PACK_S):
        hp_ref[_slab(tm, s), :] = _pack_pair(h_hi32[:, 256 * s:256 * s + 128],
                                             h_hi32[:, 256 * s + 128:256 * s + 256])

    wr = wrt_ref[...]
    w_hi = wr.astype(bf16)
    w_lo = (wr - w_hi.astype(f32)).astype(bf16)
    nt = (((1,), (1,)), ((), ()))
    lg = (lax.dot_general(w_hi, h_hi, nt, preferred_element_type=f32)
          + lax.dot_general(w_hi, h_lo, nt, preferred_element_type=f32)
          + lax.dot_general(w_lo, h_hi, nt, preferred_element_type=f32)
          + rb_ref[...])

    e_iota = lax.broadcasted_iota(i32, (N_EXP, tm), 0)
    vals, hots = [], []
    for _ in range(TOP_K):
        m = jnp.max(lg, axis=0, keepdims=True)
        idx = jnp.min(jnp.where(lg == m, e_iota, N_EXP), axis=0, keepdims=True)
        hot = e_iota == idx
        lg = jnp.where(hot, -jnp.inf, lg)
        vals.append(m)
        hots.append(hot)
    exps = [jnp.exp(v - vals[0]) for v in vals]
    denom = exps[0] + exps[1] + exps[2] + exps[3]
    gate_ref[...] = jnp.concatenate([e / denom for e in exps], axis=0)

    sel = jnp.zeros((N_EXP, tm), f32)
    for hot in hots:
        sel = sel + jnp.where(hot, 1.0, 0.0)
    sel_b = sel.astype(bf16)
    upper = jnp.where(lax.broadcasted_iota(i32, (tm, tm), 0) <= lax.broadcasted_iota(i32, (tm, tm), 1),
                      1.0, 0.0).astype(bf16)
    cum = jnp.dot(sel_b, upper, preferred_element_type=f32)
    below = jnp.where(lax.broadcasted_iota(i32, (N_EXP, N_EXP), 1)
                      < lax.broadcasted_iota(i32, (N_EXP, N_EXP), 0), 1.0, 0.0).astype(bf16)
    lower = jnp.sum(jnp.dot(below, sel_b, preferred_element_type=f32), axis=1, keepdims=True)
    place = lower + cum - sel
    lpos = [jnp.sum(jnp.where(hot, place, 0.0), axis=0, keepdims=True) for hot in hots]
    lpos_ref[...] = jnp.concatenate(lpos, axis=0).astype(i32) * PACK_S
    tcnt_ref[...] = jnp.broadcast_to(cum[:, tm - 1:tm], (N_EXP, 128))


def _outproj_router_call(l, x, y, w_out_b, g1, sc2, sh2, wrt, rb, tm):
    B, S, _ = x.shape
    T = B * S
    nj = S // tm
    tok = lambda b, j: (0, b * nj + j)
    return pl.pallas_call(
        _outproj_router_kernel,
        grid=(B, nj),
        in_specs=[pl.BlockSpec((1, tm, D), lambda b, j: (b, j, 0)),
                  pl.BlockSpec((1, tm, D), lambda b, j: (b, j, 0)),
                  _layer_resident(l, (D, D)),
                  pl.BlockSpec((1, 1, D), lambda b, j: (b, 0, 0)),
                  pl.BlockSpec((1, 1, D), lambda b, j: (b, 0, 0)),
                  pl.BlockSpec((1, 1, D), lambda b, j: (b, 0, 0)),
                  _layer_resident(l, (N_EXP, D)), _layer_resident(l, (N_EXP, 1))],
        out_specs=[pl.BlockSpec((1, tm, D), lambda b, j: (b, j, 0)),
                   pl.BlockSpec((tm * PACK_S, 128), lambda b, j: (b * nj + j, 0)),
                   pl.BlockSpec((TOP_K, tm), tok),
                   pl.BlockSpec((TOP_K, tm), tok),
                   pl.BlockSpec((None, N_EXP, 128), lambda b, j: (b * nj + j, 0, 0))],
        out_shape=[jax.ShapeDtypeStruct((B, S, D), f32),
                   jax.ShapeDtypeStruct((T * PACK_S, 128), u32),
                   jax.ShapeDtypeStruct((TOP_K, T), f32),
                   jax.ShapeDtypeStruct((TOP_K, T), i32),
                   jax.ShapeDtypeStruct((T // tm, N_EXP, 128), f32)],
        compiler_params=_params(2),
        name="out_proj_router",
    )(x, y, w_out_b, g1, sc2, sh2, wrt, rb)


def _permute_kernel(lpos_ref, h_ref, xs_ref):
    tm = h_ref.shape[0] // PACK_S
    unroll = 4

    def body(i, carry):
        for j in range(unroll):
            t = i * unroll + j
            v = h_ref[pl.ds(pl.multiple_of(t * PACK_S, PACK_S), PACK_S), :]
            for k in range(TOP_K):
                dst = pl.multiple_of(lpos_ref[k, t], PACK_S)
                xs_ref[pl.ds(dst, PACK_S), :] = v
        return carry

    lax.fori_loop(0, tm // unroll, body, 0)


def _permute_call(lpos_tiles, hp, tm):
    T = hp.shape[0] // PACK_S
    return pl.pallas_call(
        _permute_kernel,
        grid=(T // tm,),
        in_specs=[pl.BlockSpec((TOP_K, tm), lambda i: (0, i), memory_space=pltpu.SMEM),
                  pl.BlockSpec((tm * PACK_S, 128), lambda i: (i, 0))],
        out_specs=pl.BlockSpec((TOP_K * tm * PACK_S, 128), lambda i: (i, 0)),
        out_shape=jax.ShapeDtypeStruct((TOP_K * T * PACK_S, 128), u32),
        compiler_params=_params(1),
        name="tile_permute",
    )(lpos_tiles, hp)


def _expert_kernel(be_ref, nu_ref, u0_ref, nv_ref, ilo_ref, ihi_ref, pref_ref, loff_ref,
                   xs_ref, w1f_ref, b1_ref, w2f_ref, b2_ref, ys_ref, xin, xbuf, w1_ref, w2_ref, sems,
                   *, tile_rows):
    blk = pl.program_id(0)
    tm = xbuf.shape[0]
    n_used = nu_ref[0]

    def fetch(b, slot):
        e = be_ref[b]
        u0 = u0_ref[b]

        def seg(i, carry):
            s0 = pref_ref[i * N_EXP + e]
            s1 = pref_ref[(i + 1) * N_EXP + e]
            a = jnp.maximum(s0, u0)
            n = jnp.minimum(s1, u0 + tm) - a

            @pl.when(n > 0)
            def _():
                src = i * tile_rows + loff_ref[i * N_EXP + e] + (a - s0)
                pltpu.make_async_copy(xs_ref.at[_rows(src, n)], xin.at[slot, _rows(a - u0, n)],
                                      sems.at[slot]).start()
            return carry

        lax.fori_loop(ilo_ref[b], ihi_ref[b], seg, 0)

    @pl.when((blk == 0) & (n_used > 0))
    def _():
        fetch(0, 0)

    @pl.when(blk + 1 < n_used)
    def _():
        fetch(blk + 1, (blk + 1) % 2)

    @pl.when((blk < n_used) & ((blk == 0) | (be_ref[blk] != be_ref[jnp.maximum(blk - 1, 0)])))
    def _():
        w1_ref[...] = w1f_ref[...].astype(bf16)
        w2_ref[...] = w2f_ref[...].astype(bf16)

    @pl.when(blk < n_used)
    def _():
        slot = blk % 2
        n_valid = nv_ref[blk]
        pltpu.make_async_copy(xs_ref.at[_rows(0, n_valid)], xin.at[slot, _rows(0, n_valid)],
                              sems.at[slot]).wait()

        def clear(r, carry):
            xin[slot, _rows(r, 1), :] = jnp.zeros((PACK_S, 128), u32)
            return carry

        lax.fori_loop(n_valid, tm, clear, 0)
        for s in range(PACK_S):
            lo, hi = _unpack_pair(xin[slot, _slab(tm, s), :])
            xbuf[:, 256 * s:256 * s + 128] = lo.astype(bf16)
            xbuf[:, 256 * s + 128:256 * s + 256] = hi.astype(bf16)
        z = jnp.dot(xbuf[...], w1_ref[...], preferred_element_type=f32) + b1_ref[...]
        glu = jnp.minimum(z[:, 0:F_EXP], SWIGLU_LIMIT)
        lin = jnp.clip(z[:, F_EXP:], -SWIGLU_LIMIT, SWIGLU_LIMIT)
        a = glu * _sigmoid(SWIGLU_ALPHA * glu) * (lin + 1.0)
        o = jnp.dot(a.astype(bf16), w2_ref[...], preferred_element_type=f32) + b2_ref[...]
        for s in range(PACK_S):
            ys_ref[_slab(tm, s), :] = _pack_pair(o[:, 256 * s:256 * s + 128],
                                                 o[:, 256 * s + 128:256 * s + 256])

    @pl.when(blk >= n_used)
    def _():
        ys_ref[...] = jnp.zeros_like(ys_ref)


def _expert_call(l, tables, xs, w1, b1, w2, b2, n_blocks, tm, tile_rows):
    n_pref = len(tables)
    per_expert = lambda shape: pl.BlockSpec((None, None) + shape,
                                            lambda i, be, *_: (l, be[i]) + (0,) * len(shape))
    return pl.pallas_call(
        functools.partial(_expert_kernel, tile_rows=tile_rows),
        grid_spec=pltpu.PrefetchScalarGridSpec(
            num_scalar_prefetch=n_pref,
            grid=(n_blocks,),
            in_specs=[pl.BlockSpec(memory_space=pl.ANY),
                      per_expert((D, 2 * F_EXP)), per_expert((1, 2 * F_EXP)),
                      per_expert((F_EXP, D)), per_expert((1, D))],
            out_specs=pl.BlockSpec((tm * PACK_S, 128), lambda i, *_: (i, 0)),
            scratch_shapes=[pltpu.VMEM((2, tm * PACK_S, 128), u32),
                            pltpu.VMEM((tm, D), bf16),
                            pltpu.VMEM((D, 2 * F_EXP), bf16),
                            pltpu.VMEM((F_EXP, D), bf16),
                            pltpu.SemaphoreType.DMA((2,))]),
        out_shape=jax.ShapeDtypeStruct((n_blocks * tm * PACK_S, 128), u32),
        compiler_params=pltpu.CompilerParams(dimension_semantics=("arbitrary",),
                                             vmem_limit_bytes=EXPERT_VMEM_LIMIT),
        name="expert_ffn",
    )(*tables, xs, w1, b1, w2, b2)


def _combine_kernel(cnt_ref, src_ref, loff_ref, lpos_ref, gate_ref, x_ref, g2_ref, fg_ref, ys_ref,
                    o_ref, gbuf, flo, fhi, sems, *, final):
    tm = x_ref.shape[1]
    step = pl.program_id(0) * pl.num_programs(1) + pl.program_id(1)
    n_steps = pl.num_programs(0) * pl.num_programs(1)

    def fetch(tile, slot):
        for e in range(N_EXP):
            n = cnt_ref[tile * N_EXP + e]

            @pl.when(n > 0)
            def _():
                pltpu.make_async_copy(ys_ref.at[_rows(src_ref[tile * N_EXP + e], n)],
                                      gbuf.at[slot, _rows(loff_ref[tile * N_EXP + e], n)],
                                      sems.at[slot]).start()

    @pl.when(step == 0)
    def _():
        fetch(0, 0)

    for slot in range(2):
        @pl.when((step + 1 < n_steps) & ((step + 1) % 2 == slot))
        def _():
            fetch(step + 1, slot)

    unroll = 8
    for slot in range(2):
        @pl.when(step % 2 == slot)
        def _():
            pltpu.make_async_copy(ys_ref.at[_rows(0, TOP_K * tm)], gbuf.at[slot], sems.at[slot]).wait()

            def body(i, carry):
                for j in range(unroll):
                    t = i * unroll + j
                    acc_lo = jnp.zeros((PACK_S, 128), f32)
                    acc_hi = jnp.zeros((PACK_S, 128), f32)
                    for k in range(TOP_K):
                        src = pl.multiple_of(lpos_ref[k, t], PACK_S)
                        lo, hi = _unpack_pair(gbuf[slot, pl.ds(src, PACK_S), :])
                        g = gate_ref[k, t]
                        acc_lo = acc_lo + g * lo
                        acc_hi = acc_hi + g * hi
                    flo[_rows(t, 1), :] = acc_lo
                    fhi[_rows(t, 1), :] = acc_hi
                return carry

            lax.fori_loop(0, tm // unroll, body, 0)

    for s in range(PACK_S):
        for half, fref in enumerate((flo, fhi)):
            cols = slice(256 * s + 128 * half, 256 * s + 128 * half + 128)
            o_ref[0, :, cols] = x_ref[0, :, cols] + g2_ref[0, :, cols] * fref[_slab(tm, s), :]
    if final:
        o_ref[0] = _rms(o_ref[0]) * fg_ref[...]


def _combine_call(tables, lpos_tiles, gate_tiles, x, g2, final_g, ys, tm, final):
    B, S, _ = x.shape
    nj = S // tm
    smem_tile = lambda: pl.BlockSpec((TOP_K, tm), lambda b, j, *_: (0, b * nj + j),
                                     memory_space=pltpu.SMEM)
    return pl.pallas_call(
        functools.partial(_combine_kernel, final=final),
        grid_spec=pltpu.PrefetchScalarGridSpec(
            num_scalar_prefetch=len(tables),
            grid=(B, nj),
            in_specs=[smem_tile(), smem_tile(),
                      pl.BlockSpec((1, tm, D), lambda b, j, *_: (b, j, 0)),
                      pl.BlockSpec((1, 1, D), lambda b, j, *_: (b, 0, 0)),
                      pl.BlockSpec((1, D), lambda b, j, *_: (0, 0)),
                      pl.BlockSpec(memory_space=pl.ANY)],
            out_specs=pl.BlockSpec((1, tm, D), lambda b, j, *_: (b, j, 0)),
            scratch_shapes=[pltpu.VMEM((2, TOP_K * tm * PACK_S, 128), u32),
                            pltpu.VMEM((tm * PACK_S, 128), f32),
                            pltpu.VMEM((tm * PACK_S, 128), f32),
                            pltpu.SemaphoreType.DMA((2,))]),
        out_shape=jax.ShapeDtypeStruct((B, S, D), f32),
        compiler_params=_params(2),
        name="combine",
    )(*tables, lpos_tiles, gate_tiles, x, g2, final_g, ys)


def _tile(n, want):
    t = min(n, want)
    assert n % t == 0 and t % CHUNK == 0, (n, t)
    return t


def _routing_tables(tcnt, tm_tok, tm_exp, n_blocks):
    n_tiles = tcnt.shape[0]
    cnt = tcnt[:, :, 0].astype(i32)
    pref = jnp.concatenate([jnp.zeros((1, N_EXP), i32), jnp.cumsum(cnt, axis=0)])
    loff = jnp.cumsum(cnt, axis=1) - cnt
    counts = pref[-1]
    padded = (counts + tm_exp - 1) // tm_exp * tm_exp
    pad_ends = jnp.cumsum(padded)
    pad_starts = pad_ends - padded
    n_used = (pad_ends[-1:] // tm_exp).astype(i32)
    row0 = jnp.arange(n_blocks, dtype=i32) * tm_exp
    block_e = jnp.minimum(jnp.sum(row0[:, None] >= pad_ends[None, :], axis=1), N_EXP - 1).astype(i32)
    u0 = row0 - pad_starts[block_e]
    n_valid = jnp.where(row0 < pad_ends[-1], jnp.clip(counts[block_e] - u0, 0, tm_exp), 0).astype(i32)
    pref_e = pref[:, block_e].T
    tile_lo = jnp.sum(pref_e[:, 1:] <= u0[:, None], axis=1).astype(i32)
    tile_hi = jnp.sum(pref_e[:, :-1] < (u0 + tm_exp)[:, None], axis=1).astype(i32)
    expert_tables = (block_e, n_used, u0.astype(i32), n_valid, tile_lo, tile_hi,
                     pref.reshape(-1), loff.reshape(-1).astype(i32))
    src = pad_starts[None, :] + pref[:-1]
    combine_tables = (cnt.reshape(-1), src.reshape(-1).astype(i32), loff.reshape(-1).astype(i32))
    return expert_tables, combine_tables


def kernel(x, c, ada_w, ada_b, w_in, conv_dw_w, conv_dw_b, conv_ln_g, conv_ln_b, conv_pw_w,
           pool_w, pool_scale, sg_ln_g, sg_ln_b, sg_w, sg_b, mix_norm_g, w_out,
           router_w, router_b, moe_w1, moe_b1, moe_w2, moe_b2, final_g):
    B, S, d_model = x.shape
    L = ada_w.shape[0]
    assert d_model == D and w_in.shape[-1] == IN_COLS and moe_w1.shape[1] == N_EXP
    T = B * S
    tm_mix = _tile(S, 512)
    tm_tok = _tile(S, 512)
    tm_exp = _tile(S, 512)
    n_blocks = -(-(T * TOP_K) // tm_exp) + N_EXP

    w_in_b, w_out_b, pw_b, poolw_b = (w.astype(bf16) for w in (w_in, w_out, conv_pw_w, pool_w))
    rows3 = lambda a: a.reshape(L, 1, -1)
    dwb, clg, clb, pscale, slg, slb, mixg = (
        rows3(a) for a in (conv_dw_b, conv_ln_g, conv_ln_b, pool_scale, sg_ln_g, sg_ln_b, mix_norm_g))
    sgbt = jnp.swapaxes(sg_b, 1, 2)
    wrt = jnp.swapaxes(router_w, 1, 2)
    rb = router_b.reshape(L, N_EXP, 1)
    b1 = moe_b1.reshape(L, N_EXP, 1, 2 * F_EXP)
    b2 = moe_b2.reshape(L, N_EXP, 1, D)
    fg = final_g.reshape(1, D)

    mod = _ada_call(c, ada_w, ada_b)
    for l in range(L):
        sh1, sc1, g1, sh2, sc2, g2 = [m.reshape(B, 1, D) for m in jnp.split(mod[l], 6, axis=-1)]

        y = _mixer_call(l, x, sc1, sh1, w_in_b, conv_dw_w, dwb, clg, clb, pw_b, poolw_b, pscale,
                        slg, slb, sg_w, sgbt, mixg, tm_mix)
        x, hp, gates, lpos, tcnt = _outproj_router_call(l, x, y, w_out_b, g1, sc2, sh2, wrt, rb, tm_tok)

        expert_tables, combine_tables = _routing_tables(tcnt, tm_tok, tm_exp, n_blocks)
        xs = _permute_call(lpos, hp, tm_tok)
        ys = _expert_call(l, expert_tables, xs, moe_w1, b1, moe_w2, b2, n_blocks, tm_exp,
                          TOP_K * tm_tok)
        x = _combine_call(combine_tables, lpos, gates, x, g2, fg, ys, tm_tok, final=(l == L - 1))
    return x
```
